```python
import math
import jax
import jax.numpy as jnp
from jax import lax
import numpy as np

D_MODEL = 1024
BATCH = 32
SEQ = 2048
DEPTH = 2

GRID_W = 64
CTX_LEN = 256
ROPE_BASE = 10000.0
LN_EPS = 1e-6
RMS_EPS = 1e-6

RET_HEADS = 4
RET_HEAD_DIM = 128
RET_WIDTH = RET_HEADS * RET_HEAD_DIM
RET_CHUNK = 128

MLA_HEADS = 4
MLA_Q_RANK = 384
MLA_KV_RANK = 256
MLA_NOPE = 128
MLA_ROPE = 64
MLA_V = 128
MLA_WIDTH = MLA_HEADS * MLA_V
ATTN_BLOCK = 128

LRU_WIDTH = 512
LRU_BLOCKS = 8
LRU_BLOCK = LRU_WIDTH // LRU_BLOCKS
LRU_C = 8.0
LRU_CONV = 4

SSD_HEADS = 8
SSD_HEAD_DIM = 64
SSD_WIDTH = SSD_HEADS * SSD_HEAD_DIM
SSD_GROUPS = 2
SSD_STATE = 128
SSD_CONV = 4
SSD_CHUNK = 128
SSD_XBC = SSD_WIDTH + 2 * SSD_GROUPS * SSD_STATE

N_BRANCH = 4
BRANCH_WIDTH = 512
D_FF = 2816
FFN_CONV = 3

DEEPNORM_ALPHA = (2 * DEPTH) ** 0.25
DEEPNORM_BETA = (8 * DEPTH) ** -0.25

IN_SPLITS = (N_BRANCH * D_MODEL, RET_WIDTH, RET_WIDTH, RET_WIDTH, RET_WIDTH, MLA_Q_RANK, MLA_KV_RANK, MLA_ROPE, LRU_WIDTH, LRU_WIDTH, SSD_WIDTH, SSD_XBC, 2 * SSD_HEADS)
IN_WIDTH = sum(IN_SPLITS)

kernel_name = 'hybrid_bidir_diffusion_trunk'


def layer_norm(x):
    xf = x.astype(jnp.float32)
    mu = jnp.mean(xf, axis=-1, keepdims=True)
    var = jnp.mean(jnp.square(xf - mu), axis=-1, keepdims=True)
    return ((xf - mu) * lax.rsqrt(var + LN_EPS)).astype(x.dtype)


def rms_norm(x, w):
    xf = x.astype(jnp.float32)
    y = xf * lax.rsqrt(jnp.mean(jnp.square(xf), axis=-1, keepdims=True) + RMS_EPS)
    return (y * w.astype(jnp.float32)).astype(x.dtype)


def flip(t):
    return jnp.flip(t, axis=1)


def centred_conv(x, w, b):
    k, ch = w.shape
    left = (k - 1) // 2
    y = lax.conv_general_dilated(x, w[:, None, :].astype(x.dtype), window_strides=(1,), padding=[(left, k - 1 - left)], dimension_numbers=('NWC', 'WIO', 'NWC'), feature_group_count=ch)
    return y + b.astype(x.dtype)


def axial_rope(rows, dim):
    r, col = jnp.meshgrid(jnp.arange(rows, dtype=jnp.float32), jnp.arange(GRID_W, dtype=jnp.float32), indexing='ij')
    quarter = dim // 4
    inv = ROPE_BASE ** (-jnp.arange(quarter, dtype=jnp.float32) / quarter)
    ang = jnp.concatenate([r.reshape(-1, 1) * inv, col.reshape(-1, 1) * inv], axis=-1)
    return jnp.cos(ang), jnp.sin(ang)


def apply_rope(x, cos, sin):
    half = x.shape[-1] // 2
    x1, x2 = x[..., :half], x[..., half:]
    cos = cos[:, None, :].astype(x.dtype)
    sin = sin[:, None, :].astype(x.dtype)
    return jnp.concatenate([x1 * cos - x2 * sin, x1 * sin + x2 * cos], axis=-1)


def modulate(h, shift, scale):
    return layer_norm(h) * (1.0 + scale) + shift


def post_norm(x, delta, w, b):
    return layer_norm(DEEPNORM_ALPHA * x + delta) * w + b


def retention_scan(q, k, v, log_g, s0):
    b, t, h, dk = q.shape
    dv = v.shape[-1]
    n, cs = t // RET_CHUNK, RET_CHUNK
    f32 = jnp.float32
    qc = q.reshape(b, n, cs, h, dk).astype(f32)
    kc = k.reshape(b, n, cs, h, dk).astype(f32)
    vc = v.reshape(b, n, cs, h, dv).astype(f32)
    pos = jnp.arange(cs, dtype=f32)
    diff = pos[:, None] - pos[None, :]
    decay = jnp.where((diff >= 0)[..., None], jnp.exp(jnp.maximum(diff, 0.0)[..., None] * log_g), 0.0)
    scores = jnp.einsum('bnihd,bnjhd->bnhij', qc, kc) * decay.transpose(2, 0, 1)
    y = jnp.einsum('bnhij,bnjhe->bnihe', scores, vc)
    q_dec = jnp.exp((pos + 1.0)[:, None] * log_g)
    k_dec = jnp.exp((cs - 1.0 - pos)[:, None] * log_g)
    chunk_kv = jnp.einsum('bnjhd,bnjhe->nbhde', kc * k_dec[:, :, None], vc)
    chunk_dec = jnp.exp(cs * log_g)[:, None, None]

    def step(s, kv):
        return s * chunk_dec + kv, s

    s_last, s_prev = lax.scan(step, s0.astype(f32), chunk_kv)
    y = y + jnp.einsum('bnihd,nbhde->bnihe', qc * q_dec[:, :, None], s_prev)
    return y.reshape(b, t, h, dv), s_last


def retention_branch(ctx_in, lat_in, rope, decay_logit, gn_w, gn_b):
    cos, sin = rope
    log_g = jax.nn.log_sigmoid(decay_logit.astype(jnp.float32))
    scale = RET_HEAD_DIM ** -0.5

    def heads(t):
        return t.reshape(t.shape[0], t.shape[1], RET_HEADS, RET_HEAD_DIM)

    qc, kc, vc, gc = ctx_in
    qx, kx, vx, gx = lat_in
    qc, kc, vc = heads(qc), heads(kc) * scale, heads(vc)
    qx, kx, vx = apply_rope(heads(qx), cos, sin), apply_rope(heads(kx), cos, sin) * scale, heads(vx)
    s0 = jnp.zeros((qc.shape[0], RET_HEADS, RET_HEAD_DIM, RET_HEAD_DIM), jnp.float32)
    yc_f, sc_f = retention_scan(qc, kc, vc, log_g[0], s0)
    yc_b, sc_b = retention_scan(flip(qc), flip(kc), flip(vc), log_g[1], s0)
    yx_f, _ = retention_scan(qx, kx, vx, log_g[0], sc_f)
    yx_b, _ = retention_scan(flip(qx), flip(kx), flip(vx), log_g[1], sc_b)

    def finish(y, g):
        y = layer_norm(y).reshape(g.shape) * gn_w + gn_b
        return (jax.nn.silu(g) * y.astype(g.dtype)).astype(g.dtype)

    return finish(yc_f + flip(yc_b), gc), finish(yx_f + flip(yx_b), gx)


def softmax_attend(q, k, v, scale):
    s = jnp.einsum('bqhd,bkhd->bhqk', q, k, preferred_element_type=jnp.float32) * scale
    p = jax.nn.softmax(s, axis=-1).astype(v.dtype)
    return jnp.einsum('bhqk,bkhd->bqhd', p, v)


def blocked_attend(q, k, v, scale):
    b, t, h, d = q.shape
    nb = t // ATTN_BLOCK
    qb = q.reshape(b, nb, ATTN_BLOCK, h, d).swapaxes(0, 1)
    out = lax.map(lambda qq: softmax_attend(qq, k, v, scale), qb)
    return out.swapaxes(0, 1).reshape(b, t, h, v.shape[-1])


def mla_branch(ctx_in, lat_in, rope, q_norm_w, w_uq, kv_norm_w, w_ukv):
    cos, sin = rope

    def qkv(cq, ckv, kr, rotate):
        b, t, _ = cq.shape
        q = (rms_norm(cq, q_norm_w) @ w_uq).reshape(b, t, MLA_HEADS, MLA_NOPE + MLA_ROPE)
        kv = (rms_norm(ckv, kv_norm_w) @ w_ukv).reshape(b, t, MLA_HEADS, MLA_NOPE + MLA_V)
        q_nope, q_rope = q[..., :MLA_NOPE], q[..., MLA_NOPE:]
        k_nope, v = kv[..., :MLA_NOPE], kv[..., MLA_NOPE:]
        k_rope = kr[:, :, None, :]
        if rotate:
            q_rope = apply_rope(q_rope, cos, sin)
            k_rope = apply_rope(k_rope, cos, sin)
        k_rope = jnp.broadcast_to(k_rope, (b, t, MLA_HEADS, MLA_ROPE))
        return jnp.concatenate([q_nope, q_rope], -1), jnp.concatenate([k_nope, k_rope], -1), v

    qc, kc, vc = qkv(*ctx_in, rotate=False)
    qx, kx, vx = qkv(*lat_in, rotate=True)
    scale = (MLA_NOPE + MLA_ROPE) ** -0.5
    oc = softmax_attend(qc, kc, vc, scale)
    ox = blocked_attend(qx, jnp.concatenate([kc, kx], 1), jnp.concatenate([vc, vx], 1), scale)
    return oc.reshape(oc.shape[0], oc.shape[1], MLA_WIDTH), ox.reshape(ox.shape[0], ox.shape[1], MLA_WIDTH)


def linear_scan(a, b, h0):
    def combine(l, r):
        return l[0] * r[0], r[0] * l[1] + r[1]

    a_cum, h = lax.associative_scan(combine, (a, b), axis=1)
    h = h + a_cum * h0[:, None, :]
    return h, h[:, -1]


def rglru_branch(ctx_in, lat_in, conv_w, conv_b, gate_w, gate_b, lam):
    f32 = jnp.float32

    def gates(u, d):
        b, t, w = u.shape
        ub = u.reshape(b, t, LRU_BLOCKS, LRU_BLOCK).astype(f32)
        z = jnp.einsum('btnc,gncd->gbtnd', ub, gate_w[d].astype(f32)).reshape(2, b, t, w) + gate_b[d].astype(f32)[:, None, None, :]
        r, i = jax.nn.sigmoid(z[0]), jax.nn.sigmoid(z[1])
        log_a = -LRU_C * r * jax.nn.softplus(-lam[d].astype(f32))
        return jnp.exp(log_a), jnp.sqrt(-jnp.expm1(2.0 * log_a)) * (i * u.astype(f32))

    def run(u, d, h0, reverse):
        a, bb = gates(u, d)
        if reverse:
            a, bb = flip(a), flip(bb)
        h, h_last = linear_scan(a, bb, h0)
        return (flip(h) if reverse else h), h_last

    xc, gc = ctx_in
    xx, gx = lat_in
    uc = centred_conv(xc, conv_w, conv_b)
    ux = centred_conv(xx, conv_w, conv_b)
    h0 = jnp.zeros((xc.shape[0], LRU_WIDTH), f32)
    hcf, scf = run(uc, 0, h0, False)
    hcb, scb = run(uc, 1, h0, True)
    hxf, _ = run(ux, 0, scf, False)
    hxb, _ = run(ux, 1, scb, True)
    yc = ((hcf + hcb) * jax.nn.gelu(gc.astype(f32))).astype(gc.dtype)
    yx = ((hxf + hxb) * jax.nn.gelu(gx.astype(f32))).astype(gx.dtype)
    return yc, yx


def ssd_scan(x, dt, a, bm, cm, s0):
    b, t, h, p = x.shape
    g, n = bm.shape[2], bm.shape[3]
    hg, nc, cs = h // g, t // SSD_CHUNK, SSD_CHUNK
    f32 = jnp.float32
    xs = x.reshape(b, nc, cs, g, hg, p).astype(f32)
    dts = dt.reshape(b, nc, cs, g, hg).astype(f32)
    bs = bm.reshape(b, nc, cs, g, n).astype(f32)
    cs_ = cm.reshape(b, nc, cs, g, n).astype(f32)
    a_cum = jnp.cumsum(dts * a.reshape(g, hg), axis=2)
    seg = a_cum[:, :, :, None] - a_cum[:, :, None, :]
    causal = jnp.tril(jnp.ones((cs, cs), bool))[:, :, None, None]
    lmat = jnp.exp(jnp.where(causal, seg, -jnp.inf))
    cb = jnp.einsum('bnigd,bnjgd->bnijg', cs_, bs)
    w_intra = cb[..., None] * lmat * dts[:, :, None]
    y = jnp.einsum('bnijgh,bnjghp->bnighp', w_intra, xs)
    decay_end = jnp.exp(a_cum[:, :, -1:] - a_cum)
    states = jnp.einsum('bncgd,bncghp->nbghpd', bs, xs * (decay_end * dts)[..., None])
    chunk_dec = jnp.exp(a_cum[:, :, -1]).swapaxes(0, 1)

    def step(s, inp):
        st, dec = inp
        return s * dec[..., None, None] + st, s

    s_last, s_prev = lax.scan(step, s0.astype(f32).reshape(b, g, hg, p, n), (states, chunk_dec))
    y = y + jnp.einsum('bncgd,nbghpd->bncghp', cs_, s_prev) * jnp.exp(a_cum)[..., None]
    return y.reshape(b, t, h, p), s_last.reshape(b, h, p, n)


def ssd_branch(ctx_in, lat_in, conv_w, conv_b, dt_bias, a_log, d_skip, norm_w):
    f32 = jnp.float32
    a = -jnp.exp(a_log.astype(f32))
    gn = SSD_GROUPS * SSD_STATE

    def prep(z, xbc, dt_raw):
        b, t, _ = z.shape
        xbc = jax.nn.silu(centred_conv(xbc, conv_w, conv_b))
        xs = xbc[..., :SSD_WIDTH].reshape(b, t, SSD_HEADS, SSD_HEAD_DIM)
        bm = xbc[..., SSD_WIDTH:SSD_WIDTH + gn].reshape(b, t, SSD_GROUPS, SSD_STATE)
        cm = xbc[..., SSD_WIDTH + gn:].reshape(b, t, SSD_GROUPS, SSD_STATE)
        dt = jax.nn.softplus(dt_raw.astype(f32).reshape(b, t, 2, SSD_HEADS) + dt_bias.astype(f32))
        return xs, bm, cm, dt

    def run(xs, bm, cm, dt, d, s_init, reverse):
        dtd = dt[:, :, d]
        if reverse:
            xs, bm, cm, dtd = flip(xs), flip(bm), flip(cm), flip(dtd)
        y, s = ssd_scan(xs, dtd, a[d], bm, cm, s_init)
        return (flip(y) if reverse else y), s

    def finish(z, xs, yf, yb):
        b, t, _ = z.shape
        y = yf + yb + d_skip.astype(f32)[:, None] * xs.astype(f32)
        y = y.reshape(b, t, SSD_WIDTH) * jax.nn.silu(z.astype(f32))
        return rms_norm(y, norm_w).astype(z.dtype)

    zc, zx = ctx_in[0], lat_in[0]
    xs_c, bm_c, cm_c, dt_c = prep(*ctx_in)
    xs_x, bm_x, cm_x, dt_x = prep(*lat_in)
    s0 = jnp.zeros((zc.shape[0], SSD_HEADS, SSD_HEAD_DIM, SSD_STATE), f32)
    ycf, scf = run(xs_c, bm_c, cm_c, dt_c, 0, s0, False)
    ycb, scb = run(xs_c, bm_c, cm_c, dt_c, 1, s0, True)
    yxf, _ = run(xs_x, bm_x, cm_x, dt_x, 0, scf, False)
    yxb, _ = run(xs_x, bm_x, cm_x, dt_x, 1, scb, True)
    return finish(zc, xs_c, ycf, ycb), finish(zx, xs_x, yxf, yxb)


def token_mixer(h_ctx, h_lat, rope_ret, rope_mla, w_in, ret_decay, ret_gn_w, ret_gn_b, mla_q_norm, mla_w_uq, mla_kv_norm, mla_w_ukv, lru_conv_w, lru_conv_b, lru_gate_w, lru_gate_b, lru_lambda, ssd_conv_w, ssd_conv_b, ssd_dt_bias, ssd_a_log, ssd_d, ssd_norm_w, w_branch, w_out, with_ctx):
    offsets = [int(o) for o in np.cumsum(IN_SPLITS)[:-1]]
    pc = jnp.split(h_ctx @ w_in, offsets, axis=-1)
    px = jnp.split(h_lat @ w_in, offsets, axis=-1)
    ret = retention_branch(pc[1:5], px[1:5], rope_ret, ret_decay, ret_gn_w, ret_gn_b)
    mla = mla_branch(pc[5:8], px[5:8], rope_mla, mla_q_norm, mla_w_uq, mla_kv_norm, mla_w_ukv)
    lru = rglru_branch(pc[8:10], px[8:10], lru_conv_w, lru_conv_b, lru_gate_w, lru_gate_b, lru_lambda)
    ssd = ssd_branch(pc[10:13], px[10:13], ssd_conv_w, ssd_conv_b, ssd_dt_bias, ssd_a_log, ssd_d, ssd_norm_w)

    def merge(gate_logits, outs):
        gates = jax.nn.sigmoid(gate_logits.astype(jnp.float32)).astype(gate_logits.dtype)
        acc = gates[..., :D_MODEL] * (outs[0] @ w_branch[0])
        for i in range(1, N_BRANCH):
            acc = acc + gates[..., i * D_MODEL:(i + 1) * D_MODEL] * (outs[i] @ w_branch[i])
        return acc @ w_out

    out_lat = merge(px[0], (ret[1], mla[1], lru[1], ssd[1]))
    out_ctx = merge(pc[0], (ret[0], mla[0], lru[0], ssd[0])) if with_ctx else None
    return out_ctx, out_lat


def conv_ffn(h, w_up, conv_w, conv_b, w_down):
    u = centred_conv(h @ w_up, conv_w, conv_b)
    g, v = u[..., :D_FF], u[..., D_FF:]
    return (jax.nn.silu(g) * v) @ w_down


def setup_inputs(seed: int = 0) -> dict:
    key = jax.random.key(seed)
    keys = iter(jax.random.split(key, 48))
    f32 = jnp.float32
    L, D = DEPTH, D_MODEL

    def normal(shape, scale):
        return jax.random.normal(next(keys), shape, f32) * scale

    def uniform(shape, lo, hi):
        return jax.random.uniform(next(keys), shape, f32, lo, hi)

    def gain(shape):
        return 1.0 + normal(shape, 0.02)

    ret_gamma_logit = jnp.log(2.0 ** (5.0 + jnp.arange(RET_HEADS, dtype=f32)) - 1.0)
    lru_a = uniform((L, 2, LRU_WIDTH), 0.9, 0.999) ** (1.0 / LRU_C)
    dt0 = jnp.exp(uniform((L, 2, SSD_HEADS), math.log(1e-3), math.log(1e-1)))
    return {
        'x': normal((BATCH, SEQ, D), 1.0),
        'c': normal((BATCH, D), 1.0),
        'ctx': normal((BATCH, CTX_LEN, D), 1.0),
        'c_ctx': normal((D,), 1.0),
        'ada_w': normal((L, D, 6 * D), 0.5 * D ** -0.5),
        'ada_b': normal((L, 6 * D), 0.02),
        'w_in': normal((L, D, IN_WIDTH), D ** -0.5),
        'ret_decay': ret_gamma_logit + normal((L, 2, RET_HEADS), 0.1),
        'ret_gn_w': gain((L, RET_WIDTH)),
        'ret_gn_b': normal((L, RET_WIDTH), 0.02),
        'mla_q_norm': gain((L, MLA_Q_RANK)),
        'mla_w_uq': normal((L, MLA_Q_RANK, MLA_HEADS * (MLA_NOPE + MLA_ROPE)), MLA_Q_RANK ** -0.5),
        'mla_kv_norm': gain((L, MLA_KV_RANK)),
        'mla_w_ukv': normal((L, MLA_KV_RANK, MLA_HEADS * (MLA_NOPE + MLA_V)), MLA_KV_RANK ** -0.5),
        'lru_conv_w': normal((L, LRU_CONV, LRU_WIDTH), LRU_CONV ** -0.5),
        'lru_conv_b': normal((L, LRU_WIDTH), 0.02),
        'lru_gate_w': normal((L, 2, 2, LRU_BLOCKS, LRU_BLOCK, LRU_BLOCK), LRU_BLOCK ** -0.5),
        'lru_gate_b': normal((L, 2, 2, LRU_WIDTH), 0.02),
        'lru_lambda': jnp.log(lru_a) - jnp.log1p(-lru_a),
        'ssd_conv_w': normal((L, SSD_CONV, SSD_XBC), SSD_CONV ** -0.5),
        'ssd_conv_b': normal((L, SSD_XBC), 0.02),
        'ssd_dt_bias': dt0 + jnp.log(-jnp.expm1(-dt0)),
        'ssd_a_log': jnp.log(uniform((L, 2, SSD_HEADS), 1.0, 16.0)),
        'ssd_d': gain((L, SSD_HEADS)),
        'ssd_norm_w': gain((L, SSD_WIDTH)),
        'w_branch': normal((L, N_BRANCH, BRANCH_WIDTH, D), DEEPNORM_BETA * BRANCH_WIDTH ** -0.5),
        'w_out': normal((L, D, D), DEEPNORM_BETA * D ** -0.5),
        'ln1_w': gain((L, D)),
        'ln1_b': normal((L, D), 0.02),
        'ffn_w_up': normal((L, D, 2 * D_FF), D ** -0.5),
        'ffn_conv_w': normal((L, FFN_CONV, 2 * D_FF), FFN_CONV ** -0.5),
        'ffn_conv_b': normal((L, 2 * D_FF), 0.02),
        'ffn_w_down': normal((L, D_FF, D), DEEPNORM_BETA * D_FF ** -0.5),
        'ln2_w': gain((L, D)),
        'ln2_b': normal((L, D), 0.02),
    }


def reference(x, c, ctx, c_ctx, ada_w, ada_b, w_in, ret_decay, ret_gn_w, ret_gn_b, mla_q_norm, mla_w_uq, mla_kv_norm, mla_w_ukv, lru_conv_w, lru_conv_b, lru_gate_w, lru_gate_b, lru_lambda, ssd_conv_w, ssd_conv_b, ssd_dt_bias, ssd_a_log, ssd_d, ssd_norm_w, w_branch, w_out, ln1_w, ln1_b, ffn_w_up, ffn_conv_w, ffn_conv_b, ffn_w_down, ln2_w, ln2_b):
    rows = x.shape[1] // GRID_W
    rope_ret = axial_rope(rows, RET_HEAD_DIM)
    rope_mla = axial_rope(rows, MLA_ROPE)
    silu_c = jax.nn.silu(c)
    silu_cc = jax.nn.silu(c_ctx)
    h_lat, h_ctx = x, ctx
    for l in range(DEPTH):
        with_ctx = l < DEPTH - 1
        mod_x = jnp.split((silu_c @ ada_w[l] + ada_b[l])[:, None, :], 6, axis=-1)
        mod_c = jnp.split((silu_cc @ ada_w[l] + ada_b[l])[None, None, :], 6, axis=-1)
        o_ctx, o_lat = token_mixer(modulate(h_ctx, mod_c[0], mod_c[1]), modulate(h_lat, mod_x[0], mod_x[1]), rope_ret, rope_mla, w_in[l], ret_decay[l], ret_gn_w[l], ret_gn_b[l], mla_q_norm[l], mla_w_uq[l], mla_kv_norm[l], mla_w_ukv[l], lru_conv_w[l], lru_conv_b[l], lru_gate_w[l], lru_gate_b[l], lru_lambda[l], ssd_conv_w[l], ssd_conv_b[l], ssd_dt_bias[l], ssd_a_log[l], ssd_d[l], ssd_norm_w[l], w_branch[l], w_out[l], with_ctx)
        new_lat = post_norm(h_lat, mod_x[2] * o_lat, ln1_w[l], ln1_b[l])
        f_lat = conv_ffn(modulate(new_lat, mod_x[3], mod_x[4]), ffn_w_up[l], ffn_conv_w[l], ffn_conv_b[l], ffn_w_down[l])
        new_lat = post_norm(new_lat, mod_x[5] * f_lat, ln2_w[l], ln2_b[l])
        if with_ctx:
            new_ctx = post_norm(h_ctx, mod_c[2] * o_ctx, ln1_w[l], ln1_b[l])
            f_ctx = conv_ffn(modulate(new_ctx, mod_c[3], mod_c[4]), ffn_w_up[l], ffn_conv_w[l], ffn_conv_b[l], ffn_w_down[l])
            h_ctx = post_norm(new_ctx, mod_c[5] * f_ctx, ln2_w[l], ln2_b[l])
        h_lat = new_lat
    return h_lat
```

```python
import functools
import math

import jax
import jax.numpy as jnp
import numpy as np
from jax import lax
from jax.experimental import pallas as pl
from jax.experimental.pallas import tpu as pltpu

F32 = jnp.float32
BF16 = jnp.bfloat16

D_MODEL = 1024
GRID_W = 64
ROPE_BASE = 10000.0
LN_EPS = 1e-6
RMS_EPS = 1e-6

RET_HEADS = 4
RET_HEAD_DIM = 128
RET_WIDTH = RET_HEADS * RET_HEAD_DIM

MLA_HEADS = 4
MLA_Q_RANK = 384
MLA_KV_RANK = 256
MLA_NOPE = 128
MLA_ROPE = 64
MLA_V = 128
MLA_WIDTH = MLA_HEADS * MLA_V
MLA_QK_PAD = 256

LRU_WIDTH = 512
LRU_BLOCKS = 8
LRU_BLOCK = LRU_WIDTH // LRU_BLOCKS
LRU_C = 8.0

SSD_HEADS = 8
SSD_HEAD_DIM = 64
SSD_WIDTH = SSD_HEADS * SSD_HEAD_DIM
SSD_GROUPS = 2
SSD_STATE = 128
SSD_XBC = SSD_WIDTH + 2 * SSD_GROUPS * SSD_STATE
SSD_HG = SSD_HEADS // SSD_GROUPS

N_BRANCH = 4
D_FF = 2816

CHUNK = 128
LANES = 128
SUBLANES = 8
ROW_TILE = 256
N_CHUNK = 512
FF_CHUNK = 256
VMEM_LIMIT = 56 * 1024 * 1024

SEG_GATES = N_BRANCH * D_MODEL
SEG_RET = 4 * RET_WIDTH
SEG_SSD = SSD_XBC + SSD_WIDTH
SEG_LRU = 2 * LRU_WIDTH
SEG_MLA = 768
SEGS = (SEG_GATES, SEG_RET, SEG_SSD, SEG_LRU, SEG_MLA)
DT_COLS = SSD_GROUPS * LANES


def _dot(a, b):
    return jnp.dot(a, b, preferred_element_type=F32)


def _dot_nt(a, b):
    return lax.dot_general(a, b, (((1,), (1,)), ((), ())), preferred_element_type=F32)


def _split3(x):
    x1 = x.astype(BF16)
    r1 = x - x1.astype(F32)
    x2 = r1.astype(BF16)
    x3 = (r1 - x2.astype(F32)).astype(BF16)
    return x1, x2, x3


def _dot3_l(m, x):
    x1, x2, x3 = _split3(x)
    return _dot(m, x1) + _dot(m, x2) + _dot(m, x3)


def _dot3_r(x, m):
    x1, x2, x3 = _split3(x)
    return _dot(x1, m) + _dot(x2, m) + _dot(x3, m)


def _ln(x):
    mu = jnp.mean(x, axis=-1, keepdims=True)
    xc = x - mu
    var = jnp.mean(xc * xc, axis=-1, keepdims=True)
    return xc * lax.rsqrt(var + LN_EPS)


def _rms(x):
    return x * lax.rsqrt(jnp.mean(x * x, axis=-1, keepdims=True) + RMS_EPS)


def _sigmoid(x):
    return 1.0 / (1.0 + jnp.exp(-x))


def _silu(x):
    return x * _sigmoid(x)


def _softplus(x):
    return jnp.maximum(x, 0.0) + jnp.log1p(jnp.exp(-jnp.abs(x)))


def _gelu_tanh(x):
    return 0.5 * x * (1.0 + jnp.tanh(math.sqrt(2.0 / math.pi) * (x + 0.044715 * (x * x * x))))


def _iota(shape, dim):
    return lax.broadcasted_iota(jnp.int32, shape, dim)


def _resident(shape):
    nd = len(shape)
    return pl.BlockSpec(shape, lambda *_: (0,) * nd, pipeline_mode=pl.Buffered(1))


def _params(*sem):
    return pltpu.CompilerParams(dimension_semantics=sem, vmem_limit_bytes=VMEM_LIMIT)


def _mod_kernel(c_ref, w_ref, b_ref, o_ref):
    s = _silu(c_ref[...])
    o_ref[...] = _dot(s.astype(BF16), w_ref[...]) + b_ref[...]


def _mod_call(cc, w, b):
    rows = cc.shape[0]
    n = w.shape[1]
    return pl.pallas_call(
        _mod_kernel,
        grid=(n // D_MODEL,),
        in_specs=[
            pl.BlockSpec((rows, D_MODEL), lambda j: (0, 0)),
            pl.BlockSpec((D_MODEL, D_MODEL), lambda j: (0, j)),
            pl.BlockSpec((1, D_MODEL), lambda j: (0, j)),
        ],
        out_specs=pl.BlockSpec((rows, D_MODEL), lambda j: (0, j)),
        out_shape=jax.ShapeDtypeStruct((rows, n), F32),
        compiler_params=_params("arbitrary"),
        name="adaln_mod",
    )(cc, w, b)


def _pick_mod(is_ctx, mx_ref, mc_ref, row):
    return jnp.where(is_ctx, mc_ref[0, row:row + 1, :], mx_ref[0, row:row + 1, :])


def _inproj_kernel(h_ref, mx_ref, mc_ref, w_ref, wdt_ref, og, orr, os_, ol, om, odt, *, ctx):
    tm = h_ref.shape[1]
    is_ctx = pl.program_id(1) * tm < ctx
    shift = _pick_mod(is_ctx, mx_ref, mc_ref, 0)
    scale = _pick_mod(is_ctx, mx_ref, mc_ref, 1)
    xm = (_ln(h_ref[0]) * (1.0 + scale) + shift).astype(BF16)
    col = 0
    for oref, width in zip((og, orr, os_, ol, om), SEGS):
        for c0 in range(0, width, N_CHUNK):
            cw = min(N_CHUNK, width - c0)
            oref[0, :, c0:c0 + cw] = _dot(xm, w_ref[:, col + c0:col + c0 + cw]).astype(oref.dtype)
        col += width
    odt[0] = _dot(xm, wdt_ref[...])


def _inproj_call(h, mods, w_cat, w_dt, ctx):
    b, s, _ = h.shape
    tm = ROW_TILE
    nb = mods.shape[0] - 1
    tok = lambda w: pl.BlockSpec((1, tm, w), lambda i, t: (i, t, 0))
    outs = [jax.ShapeDtypeStruct((b, s, w), BF16) for w in SEGS] + [jax.ShapeDtypeStruct((b, s, DT_COLS), F32)]
    return pl.pallas_call(
        functools.partial(_inproj_kernel, ctx=ctx),
        grid=(b, s // tm),
        in_specs=[
            tok(D_MODEL),
            pl.BlockSpec((1, 8, D_MODEL), lambda i, t: (i, 0, 0)),
            pl.BlockSpec((1, 8, D_MODEL), lambda i, t: (nb, 0, 0)),
            _resident(w_cat.shape),
            _resident(w_dt.shape),
        ],
        out_specs=[tok(w) for w in SEGS] + [tok(DT_COLS)],
        out_shape=outs,
        compiler_params=_params("parallel", "parallel"),
        name="in_proj",
    )(h, mods, mods, w_cat, w_dt)


def _ret_kernel(q_ref, k_ref, v_ref, g_ref, cos_ref, sin_ref, lg_ref, gw_ref, gb_ref, o_ref, y_scr, *, ctx):
    s_len = q_ref.shape[1]
    nck, nctx = s_len // CHUNK, ctx // CHUNK
    scale = RET_HEAD_DIM ** -0.5
    lgf = lg_ref[0, 0:1, :]
    lgb = lg_ref[0, 1:2, :]
    d = (_iota((CHUNK, CHUNK), 0) - _iota((CHUNK, CHUNK), 1)).astype(F32)
    dec = (jnp.where(d >= 0, jnp.exp(jnp.maximum(d, 0.0) * lgf), 0.0)
           + jnp.where(d <= 0, jnp.exp(jnp.maximum(-d, 0.0) * lgb), 0.0))
    pos = _iota((CHUNK, 1), 0).astype(F32)
    qdec_f = jnp.exp((pos + 1.0) * lgf)
    kdec_f = jnp.exp((CHUNK - 1.0 - pos) * lgf)
    qdec_b = jnp.exp((CHUNK - pos) * lgb)
    kdec_b = jnp.exp(pos * lgb)
    cdec_f = jnp.exp(CHUNK * lgf)
    cdec_b = jnp.exp(CHUNK * lgb)
    gw = gw_ref[0]
    gb = gb_ref[0]

    def load(n):
        r = pl.ds(pl.multiple_of(n * CHUNK, CHUNK), CHUNK)
        cos = cos_ref[r, :]
        sin = sin_ref[r, :]
        q = q_ref[0, r, :].astype(F32)
        k = k_ref[0, r, :].astype(F32)
        qr = q * cos + pltpu.roll(q, RET_HEAD_DIM // 2, 1) * sin
        kr = (k * cos + pltpu.roll(k, RET_HEAD_DIM // 2, 1) * sin) * scale
        return r, qr, kr, v_ref[0, r, :]

    def state_update(st, cdec, kd, v):
        return st * cdec + _dot(kd.T.astype(BF16), v)

    def fwd(n, st):
        r, qr, kr, v = load(n)
        sc = _dot_nt(qr.astype(BF16), kr.astype(BF16)) * dec
        y_scr[r, :] = _dot(sc.astype(BF16), v) + _dot((qr * qdec_f).astype(BF16), st.astype(BF16))
        return state_update(st, cdec_f, kr * kdec_f, v)

    def bwd(n, st):
        r, qr, kr, v = load(n)
        y = y_scr[r, :] + _dot((qr * qdec_b).astype(BF16), st.astype(BF16))
        g = g_ref[0, r, :].astype(F32)
        o_ref[0, r, :] = (_silu(g) * (_ln(y) * gw + gb)).astype(o_ref.dtype)
        return state_update(st, cdec_b, kr * kdec_b, v)

    zero = jnp.zeros((RET_HEAD_DIM, RET_HEAD_DIM), F32)
    lax.fori_loop(0, nck, fwd, zero)
    st = lax.fori_loop(0, nctx, lambda i, st: bwd(nctx - 1 - i, st), zero)
    lax.fori_loop(0, nck - nctx, lambda i, st: bwd(nck - 1 - i, st), st)


def _ret_call(p_ret, cos, sin, lg, gw, gb, ctx):
    b, s, _ = p_ret.shape
    col = lambda off: pl.BlockSpec((1, s, RET_HEAD_DIM), lambda i, h: (i, 0, off + h))
    vec = pl.BlockSpec((1, 1, RET_HEAD_DIM), lambda i, h: (h, 0, 0))
    tab = pl.BlockSpec((s, RET_HEAD_DIM), lambda i, h: (0, 0))
    return pl.pallas_call(
        functools.partial(_ret_kernel, ctx=ctx),
        grid=(b, RET_HEADS),
        in_specs=[col(0), col(RET_HEADS), col(2 * RET_HEADS), col(3 * RET_HEADS), tab, tab,
                  pl.BlockSpec((1, 8, RET_HEAD_DIM), lambda i, h: (h, 0, 0)), vec, vec],
        out_specs=pl.BlockSpec((1, s, RET_HEAD_DIM), lambda i, h: (i, 0, h)),
        out_shape=jax.ShapeDtypeStruct((b, s, RET_WIDTH), BF16),
        scratch_shapes=[pltpu.VMEM((s, RET_HEAD_DIM), F32)],
        compiler_params=_params("parallel", "parallel"),
        name="retention",
    )(p_ret, p_ret, p_ret, p_ret, cos, sin, lg, gw, gb)


def _mla_prep_kernel(m_ref, qn_ref, kn_ref, wq_ref, wkv_ref, cos_ref, sin_ref, q_ref, k_ref, v_ref):
    m = m_ref[0].astype(F32)
    cq = _rms(m[:, :MLA_Q_RANK]) * qn_ref[...]
    ckv = _rms(m[:, MLA_Q_RANK:MLA_Q_RANK + MLA_KV_RANK]) * kn_ref[...]
    kr = m[:, MLA_Q_RANK + MLA_KV_RANK:]
    q = _dot(cq.astype(BF16), wq_ref[...])
    kv = _dot(ckv.astype(BF16), wkv_ref[...])
    cos = cos_ref[...]
    sin = sin_ref[...]
    first_half = (_iota(cos.shape, 1) % MLA_ROPE) < (MLA_ROPE // 2)

    def rope(x):
        swapped = jnp.where(first_half, pltpu.roll(x, LANES - MLA_ROPE // 2, 1), pltpu.roll(x, MLA_ROPE // 2, 1))
        return x * cos + swapped * sin

    scale = (MLA_NOPE + MLA_ROPE) ** -0.5
    kr = rope(kr).astype(BF16)
    for h in range(MLA_HEADS):
        c0 = h * MLA_QK_PAD
        q_ref[0, h, :, :MLA_NOPE] = (q[:, c0:c0 + MLA_NOPE] * scale).astype(BF16)
        q_ref[0, h, :, MLA_NOPE:] = (rope(q[:, c0 + MLA_NOPE:c0 + MLA_QK_PAD]) * scale).astype(BF16)
        k_ref[0, h, :, :MLA_NOPE] = kv[:, c0:c0 + MLA_NOPE].astype(BF16)
        k_ref[0, h, :, MLA_NOPE:] = kr
        v_ref[0, h] = kv[:, c0 + MLA_NOPE:c0 + MLA_NOPE + MLA_V].astype(BF16)


def _mla_prep_call(p_mla, qn, kn, wq, wkv, cos, sin):
    b, s, _ = p_mla.shape
    tm = ROW_TILE
    head = lambda w: pl.BlockSpec((1, MLA_HEADS, tm, w), lambda i, t: (i, 0, t, 0))
    tab = pl.BlockSpec((tm, LANES), lambda i, t: (t, 0))
    return pl.pallas_call(
        _mla_prep_kernel,
        grid=(b, s // tm),
        in_specs=[pl.BlockSpec((1, tm, SEG_MLA), lambda i, t: (i, t, 0)),
                  _resident(qn.shape), _resident(kn.shape), _resident(wq.shape), _resident(wkv.shape), tab, tab],
        out_specs=[head(MLA_QK_PAD), head(MLA_QK_PAD), head(MLA_V)],
        out_shape=[jax.ShapeDtypeStruct((b, MLA_HEADS, s, MLA_QK_PAD), BF16),
                   jax.ShapeDtypeStruct((b, MLA_HEADS, s, MLA_QK_PAD), BF16),
                   jax.ShapeDtypeStruct((b, MLA_HEADS, s, MLA_V), BF16)],
        compiler_params=_params("parallel", "parallel"),
        name="mla_prep",
    )(p_mla, qn, kn, wq, wkv, cos, sin)


def _attn_kernel(q_ref, k_ref, v_ref, o_ref, *, ctx, first_tile):
    tq = q_ref.shape[2]
    s_len = k_ref.shape[2]
    q = q_ref[0, 0]
    row0 = (pl.program_id(2) + first_tile) * tq

    def attend(nk):
        sc = _dot_nt(q, k_ref[0, 0, :nk, :])
        p = jnp.exp(sc - jnp.max(sc, axis=-1, keepdims=True))
        o = _dot(p.astype(BF16), v_ref[0, 0, :nk, :]) / jnp.sum(p, axis=-1, keepdims=True)
        o_ref[0] = o.astype(o_ref.dtype)

    pl.when(row0 < ctx)(lambda: attend(ctx))
    pl.when(row0 >= ctx)(lambda: attend(s_len))


def _attn_call(q, k, v, ctx, with_ctx):
    b, nh, s, _ = q.shape
    tq = ROW_TILE
    first = 0 if with_ctx else ctx // tq
    kv = lambda w: pl.BlockSpec((1, 1, s, w), lambda i, h, t: (i, h, 0, 0))
    return pl.pallas_call(
        functools.partial(_attn_kernel, ctx=ctx, first_tile=first),
        grid=(b, nh, s // tq - first),
        in_specs=[pl.BlockSpec((1, 1, tq, MLA_QK_PAD), lambda i, h, t: (i, h, t + first, 0)),
                  kv(MLA_QK_PAD), kv(MLA_V)],
        out_specs=pl.BlockSpec((1, tq, MLA_V), lambda i, h, t: (i, t + first, h)),
        out_shape=jax.ShapeDtypeStruct((b, s, MLA_WIDTH), BF16),
        compiler_params=_params("parallel", "parallel", "arbitrary"),
        name="mla_attention",
    )(q, k, v)


def _padded_row(n, nctx):
    return pl.multiple_of(n * CHUNK + jnp.where(n < nctx, SUBLANES, 2 * SUBLANES), SUBLANES)


def _fill_padded(src_ref, col, pad_scr, *, ctx):
    s_len = src_ref.shape[1]
    width = pad_scr.shape[1]
    nctx = ctx // CHUNK
    zeros = jnp.zeros((SUBLANES, width), F32)
    pad_scr[0:SUBLANES, :] = zeros
    pad_scr[ctx + SUBLANES:ctx + 2 * SUBLANES, :] = zeros
    pad_scr[s_len + 2 * SUBLANES:s_len + 3 * SUBLANES, :] = zeros

    def body(n, carry):
        r = pl.ds(pl.multiple_of(n * CHUNK, CHUNK), CHUNK)
        pad_scr[pl.ds(_padded_row(n, nctx), CHUNK), :] = src_ref[0, r, col].astype(F32)
        return carry

    lax.fori_loop(0, s_len // CHUNK, body, 0)


def _conv4(pad_scr, n, nctx, taps_ref):
    base = _padded_row(n, nctx)
    win = pad_scr[pl.ds(base - SUBLANES, CHUNK + 2 * SUBLANES), :]
    rows = CHUNK + 2 * SUBLANES
    lo, hi = SUBLANES, SUBLANES + CHUNK
    acc = taps_ref[4:5, :] + win[lo:hi] * taps_ref[1:2, :]
    acc = acc + pltpu.roll(win, 1, 0)[lo:hi] * taps_ref[0:1, :]
    acc = acc + pltpu.roll(win, rows - 1, 0)[lo:hi] * taps_ref[2:3, :]
    acc = acc + pltpu.roll(win, rows - 2, 0)[lo:hi] * taps_ref[3:4, :]
    return acc


def _lru_kernel(x_ref, g_ref, w_ref, vec_ref, o_ref, pad_scr, af_scr, bf_scr, ab_scr, bb_scr, *, ctx):
    s_len = x_ref.shape[1]
    nck, nctx = s_len // CHUNK, ctx // CHUNK
    ntile, nctile = s_len // SUBLANES, ctx // SUBLANES
    _fill_padded(x_ref, slice(None), pad_scr, ctx=ctx)
    rt = _iota((CHUNK, LANES), 0) % SUBLANES

    def gates(n, carry):
        r = pl.ds(pl.multiple_of(n * CHUNK, CHUNK), CHUNK)
        u = _conv4(pad_scr, n, nctx, vec_ref)
        z = _dot(u.astype(BF16), w_ref[0])
        for d, (a_scr, b_scr) in enumerate(((af_scr, bf_scr), (ab_scr, bb_scr))):
            rg = _sigmoid(z[:, (2 * d) * LANES:(2 * d + 1) * LANES] + vec_ref[5 + 2 * d:6 + 2 * d, :])
            ig = _sigmoid(z[:, (2 * d + 1) * LANES:(2 * d + 2) * LANES] + vec_ref[6 + 2 * d:7 + 2 * d, :])
            log_a = -LRU_C * rg * vec_ref[9 + d:10 + d, :]
            a = jnp.exp(log_a)
            bv = jnp.sqrt(-jnp.tanh(log_a) * (1.0 + a * a)) * (ig * u)
            for sh in (1, 2, 4):
                if d == 0:
                    keep = rt >= sh
                    a_s, b_s = pltpu.roll(a, sh, 0), pltpu.roll(bv, sh, 0)
                else:
                    keep = rt < SUBLANES - sh
                    a_s, b_s = pltpu.roll(a, CHUNK - sh, 0), pltpu.roll(bv, CHUNK - sh, 0)
                bv = bv + a * jnp.where(keep, b_s, 0.0)
                a = a * jnp.where(keep, a_s, 1.0)
            a_scr[r, :] = a
            b_scr[r, :] = bv
        return carry

    lax.fori_loop(0, nck, gates, 0)

    def carry_step(i, hs):
        hf, hb = hs
        rf = pl.ds(pl.multiple_of(i * SUBLANES, SUBLANES), SUBLANES)
        tf = af_scr[rf, :] * hf + bf_scr[rf, :]
        bf_scr[rf, :] = tf
        j = jnp.where(i < nctile, nctile - 1 - i, ntile - 1 - (i - nctile))
        rb = pl.ds(pl.multiple_of(j * SUBLANES, SUBLANES), SUBLANES)
        tb = ab_scr[rb, :] * hb + bb_scr[rb, :]
        bb_scr[rb, :] = tb
        return (jnp.broadcast_to(tf[SUBLANES - 1:SUBLANES, :], (SUBLANES, LANES)),
                jnp.broadcast_to(tb[0:1, :], (SUBLANES, LANES)))

    zero = jnp.zeros((SUBLANES, LANES), F32)
    lax.fori_loop(0, ntile, carry_step, (zero, zero))

    def finish(n, carry):
        r = pl.ds(pl.multiple_of(n * CHUNK, CHUNK), CHUNK)
        g = g_ref[0, r, :].astype(F32)
        o_ref[0, r, :] = ((bf_scr[r, :] + bb_scr[r, :]) * _gelu_tanh(g)).astype(o_ref.dtype)
        return carry

    lax.fori_loop(0, nck, finish, 0)


def _lru_call(p_lru, w_bd, vec, ctx):
    b, s, _ = p_lru.shape
    ncb = LRU_WIDTH // LANES
    return pl.pallas_call(
        functools.partial(_lru_kernel, ctx=ctx),
        grid=(b, ncb),
        in_specs=[pl.BlockSpec((1, s, LANES), lambda i, j: (i, 0, j)),
                  pl.BlockSpec((1, s, LANES), lambda i, j: (i, 0, ncb + j)),
                  pl.BlockSpec((1, LANES, 4 * LANES), lambda i, j: (j, 0, 0)),
                  pl.BlockSpec((16, LANES), lambda i, j: (0, j))],
        out_specs=pl.BlockSpec((1, s, LANES), lambda i, j: (i, 0, j)),
        out_shape=jax.ShapeDtypeStruct((b, s, LRU_WIDTH), BF16),
        scratch_shapes=[pltpu.VMEM((s + 3 * SUBLANES, LANES), F32)] + [pltpu.VMEM((s, LANES), F32)] * 4,
        compiler_params=_params("parallel", "parallel"),
        name="rg_lru",
    )(p_lru, p_lru, w_bd, vec)


def _expand_heads(m, lane0, rows):
    width = SSD_HG * SSD_HEAD_DIM
    lane = _iota((rows, width), 1)
    out = jnp.broadcast_to(m[:, lane0 + SSD_HG - 1:lane0 + SSD_HG], (rows, width))
    for h in range(SSD_HG - 2, -1, -1):
        out = jnp.where(lane < (h + 1) * SSD_HEAD_DIM, m[:, lane0 + h:lane0 + h + 1], out)
    return out


def _ssd_kernel(x_ref, b_ref, c_ref, z_ref, dt_ref, cvx_ref, cvb_ref, cvc_ref, hp_ref, dsk_ref, o_ref,
                padx, padb, padc, xs_scr, bs_scr, cs_scr, y_scr, *, ctx):
    s_len = x_ref.shape[1]
    nck, nctx = s_len // CHUNK, ctx // CHUNK
    gw = SSD_HG * SSD_HEAD_DIM
    _fill_padded(x_ref, slice(None), padx, ctx=ctx)
    _fill_padded(b_ref, slice(None), padb, ctx=ctx)
    _fill_padded(c_ref, slice(None), padc, ctx=ctx)

    def conv(n, carry):
        r = pl.ds(pl.multiple_of(n * CHUNK, CHUNK), CHUNK)
        xs_scr[r, :] = _silu(_conv4(padx, n, nctx, cvx_ref))
        bs_scr[r, :] = _silu(_conv4(padb, n, nctx, cvb_ref))
        cs_scr[r, :] = _silu(_conv4(padc, n, nctx, cvc_ref))
        return carry

    lax.fori_loop(0, nck, conv, 0)

    ii = _iota((CHUNK, CHUNK), 0)
    jj = _iota((CHUNK, CHUNK), 1)
    tri_l = (jj <= ii).astype(BF16)
    tri_u = (jj >= ii).astype(BF16)
    lane256 = _iota((CHUNK, gw), 1)
    dt_bias = hp_ref[0, 0:1, :]
    a_neg = hp_ref[0, 1:2, :]

    def chunk_terms(n):
        r = pl.ds(pl.multiple_of(n * CHUNK, CHUNK), CHUNK)
        dt = _softplus(dt_ref[0, r, :] + dt_bias)
        dta = dt * a_neg
        xg = xs_scr[r, :]
        bt = bs_scr[r, :].T.astype(BF16)
        cg = cs_scr[r, :].astype(BF16)
        return r, dt, dta, xg, bt, cg

    def fwd(n, st):
        r, dt, dta, xg, bt, cg = chunk_terms(n)
        af = _dot3_l(tri_l, dta)
        rv = _dot3_l(tri_u, dta)
        dta_t = dta.T
        af_t = _dot3_r(dta_t, tri_u)
        rv_t = _dot3_r(dta_t, tri_l)
        dt_t = dt.T
        cb = _dot_nt(cg, bs_scr[r, :].astype(BF16))
        y = jnp.zeros((CHUNK, gw), F32)
        for h in range(SSD_HG):
            hb = SSD_HG + h
            arg = jnp.where(ii >= jj, af[:, h:h + 1] - af_t[h:h + 1, :], rv[:, hb:hb + 1] - rv_t[hb:hb + 1, :])
            wdt = jnp.where(ii > jj, dt_t[h:h + 1, :],
                            jnp.where(ii < jj, dt_t[hb:hb + 1, :], dt_t[h:h + 1, :] + dt_t[hb:hb + 1, :]))
            w = (cb * jnp.exp(arg) * wdt).astype(BF16)
            in_head = (lane256 >= h * SSD_HEAD_DIM) & (lane256 < (h + 1) * SSD_HEAD_DIM)
            y = y + _dot(w, jnp.where(in_head, xg, 0.0).astype(BF16))
        y = y + _dot(cg, st.astype(BF16)) * _expand_heads(jnp.exp(af), 0, CHUNK)
        y_scr[r, :] = y
        af_last = af[CHUNK - 1:CHUNK, :]
        xw = xg * _expand_heads(jnp.exp(af_last - af) * dt, 0, CHUNK)
        return st * _expand_heads(jnp.exp(af_last), 0, 1) + _dot(bt, xw.astype(BF16))

    def bwd(n, st):
        r, dt, dta, xg, bt, cg = chunk_terms(n)
        rv = _dot3_l(tri_u, dta)
        y = y_scr[r, :] + _dot(cg, st.astype(BF16)) * _expand_heads(jnp.exp(rv), SSD_HG, CHUNK)
        y = y + dsk_ref[...] * xg
        o_ref[0, r, :] = (y * _silu(z_ref[0, r, :].astype(F32))).astype(o_ref.dtype)
        rv_first = rv[0:1, :]
        xw = xg * _expand_heads(jnp.exp(rv_first - rv) * dt, SSD_HG, CHUNK)
        return st * _expand_heads(jnp.exp(rv_first), SSD_HG, 1) + _dot(bt, xw.astype(BF16))

    zero = jnp.zeros((SSD_STATE, gw), F32)
    lax.fori_loop(0, nck, fwd, zero)
    st = lax.fori_loop(0, nctx, lambda i, st: bwd(nctx - 1 - i, st), zero)
    lax.fori_loop(0, nck - nctx, lambda i, st: bwd(nck - 1 - i, st), st)


def _ssd_call(p_ssd, dt, cv, hp, dsk, ctx):
    b, s, _ = p_ssd.shape
    gw = SSD_HG * SSD_HEAD_DIM
    nx = SSD_WIDTH // LANES
    nbc = SSD_GROUPS * SSD_STATE // LANES
    pad = lambda w: pltpu.VMEM((s + 3 * SUBLANES, w), F32)
    return pl.pallas_call(
        functools.partial(_ssd_kernel, ctx=ctx),
        grid=(b, SSD_GROUPS),
        in_specs=[pl.BlockSpec((1, s, gw), lambda i, g: (i, 0, g)),
                  pl.BlockSpec((1, s, LANES), lambda i, g: (i, 0, nx + g)),
                  pl.BlockSpec((1, s, LANES), lambda i, g: (i, 0, nx + nbc + g)),
                  pl.BlockSpec((1, s, gw), lambda i, g: (i, 0, SSD_XBC // gw + g)),
                  pl.BlockSpec((1, s, LANES), lambda i, g: (i, 0, g)),
                  pl.BlockSpec((8, gw), lambda i, g: (0, g)),
                  pl.BlockSpec((8, LANES), lambda i, g: (0, nx + g)),
                  pl.BlockSpec((8, LANES), lambda i, g: (0, nx + nbc + g)),
                  pl.BlockSpec((1, 8, LANES), lambda i, g: (g, 0, 0)),
                  pl.BlockSpec((1, gw), lambda i, g: (0, g))],
        out_specs=pl.BlockSpec((1, s, gw), lambda i, g: (i, 0, g)),
        out_shape=jax.ShapeDtypeStruct((b, s, SSD_WIDTH), BF16),
        scratch_shapes=[pad(gw), pad(LANES), pad(LANES),
                        pltpu.VMEM((s, gw), F32), pltpu.VMEM((s, LANES), F32), pltpu.VMEM((s, LANES), F32),
                        pltpu.VMEM((s, gw), F32)],
        compiler_params=_params("parallel", "parallel"),
        name="ssd",
    )(p_ssd, p_ssd, p_ssd, p_ssd, dt, cv, cv, cv, hp, dsk)


def _merge_kernel(g_ref, ret_ref, mla_ref, lru_ref, ssd_ref, h_ref, mx_ref, mc_ref, wb_ref, wo_ref, vec_ref, nw_ref,
                  o_ref, *, ctx, first_tile, alpha):
    tm = h_ref.shape[1]
    is_ctx = (pl.program_id(1) + first_tile) * tm < ctx
    ssd = (_rms(ssd_ref[0].astype(F32)) * nw_ref[...]).astype(BF16)
    acc = None
    for i, br in enumerate((ret_ref[0], mla_ref[0], lru_ref[0], ssd)):
        gate = _sigmoid(g_ref[0, :, i * D_MODEL:(i + 1) * D_MODEL].astype(F32))
        term = gate * _dot(br, wb_ref[i])
        acc = term if acc is None else acc + term
    o = _dot(acc.astype(BF16), wo_ref[...])
    y = _ln(alpha * h_ref[0] + _pick_mod(is_ctx, mx_ref, mc_ref, 2) * o)
    o_ref[0] = y * vec_ref[0:1, :] + vec_ref[1:2, :]


def _merge_call(p_gates, y_ret, y_mla, y_lru, y_ssd, h, mods, wb, wo, vec, nw, ctx, with_ctx, alpha):
    b, s, _ = h.shape
    tm = ROW_TILE
    first = 0 if with_ctx else ctx // tm
    nb = mods.shape[0] - 1
    tok = lambda w: pl.BlockSpec((1, tm, w), lambda i, t: (i, t + first, 0))
    return pl.pallas_call(
        functools.partial(_merge_kernel, ctx=ctx, first_tile=first, alpha=alpha),
        grid=(b, s // tm - first),
        in_specs=[tok(SEG_GATES), tok(RET_WIDTH), tok(MLA_WIDTH), tok(LRU_WIDTH), tok(SSD_WIDTH), tok(D_MODEL),
                  pl.BlockSpec((1, 8, D_MODEL), lambda i, t: (i, 0, 0)),
                  pl.BlockSpec((1, 8, D_MODEL), lambda i, t: (nb, 0, 0)),
                  _resident(wb.shape), _resident(wo.shape), _resident(vec.shape), _resident(nw.shape)],
        out_specs=tok(D_MODEL),
        out_shape=jax.ShapeDtypeStruct((b, s, D_MODEL), F32),
        compiler_params=_params("parallel", "parallel"),
        name="merge_postnorm",
    )(p_gates, y_ret, y_mla, y_lru, y_ssd, h, mods, mods, wb, wo, vec, nw)


def _ffn_kernel(h_ref, hp_ref, hn_ref, mx_ref, mc_ref, wu_ref, wd_ref, cv_ref, vec_ref, o_ref, *,
                ctx, s_len, first_tile, alpha):
    tm = h_ref.shape[1]
    row0 = (pl.program_id(1) + first_tile) * tm
    is_ctx = row0 < ctx
    shift = _pick_mod(is_ctx, mx_ref, mc_ref, 3)
    scale = _pick_mod(is_ctx, mx_ref, mc_ref, 4)
    h = h_ref[0]
    prev_ok = jnp.logical_and(row0 != 0, row0 != ctx)
    next_ok = jnp.logical_and(row0 + tm != ctx, row0 + tm != s_len)
    mod = lambda x: _ln(x) * (1.0 + scale) + shift
    xm = jnp.concatenate([jnp.where(prev_ok, mod(hp_ref[0]), 0.0), mod(h), jnp.where(next_ok, mod(hn_ref[0]), 0.0)],
                         axis=0).astype(BF16)
    rows = tm + 2 * SUBLANES
    lo, hi = SUBLANES, SUBLANES + tm

    def conv(u, c0):
        out = cv_ref[3:4, c0:c0 + FF_CHUNK] + u[lo:hi] * cv_ref[1:2, c0:c0 + FF_CHUNK]
        out = out + pltpu.roll(u, 1, 0)[lo:hi] * cv_ref[0:1, c0:c0 + FF_CHUNK]
        return out + pltpu.roll(u, rows - 1, 0)[lo:hi] * cv_ref[2:3, c0:c0 + FF_CHUNK]

    acc = jnp.zeros((tm, D_MODEL), F32)
    for c0 in range(0, D_FF, FF_CHUNK):
        g = conv(_dot(xm, wu_ref[:, c0:c0 + FF_CHUNK]), c0)
        v = conv(_dot(xm, wu_ref[:, D_FF + c0:D_FF + c0 + FF_CHUNK]), D_FF + c0)
        acc = acc + _dot((_silu(g) * v).astype(BF16), wd_ref[c0:c0 + FF_CHUNK, :])
    y = _ln(alpha * h + _pick_mod(is_ctx, mx_ref, mc_ref, 5) * acc)
    o_ref[0] = y * vec_ref[0:1, :] + vec_ref[1:2, :]


def _ffn_call(h, mods, wu, wd, cv, vec, ctx, with_ctx, alpha):
    b, s, _ = h.shape
    tm = ROW_TILE
    first = 0 if with_ctx else ctx // tm
    nb = mods.shape[0] - 1
    per = tm // SUBLANES
    lo_blk = first * per
    hi_blk = s // SUBLANES - 1
    out_rows = s - first * tm
    return pl.pallas_call(
        functools.partial(_ffn_kernel, ctx=ctx, s_len=s, first_tile=first, alpha=alpha),
        grid=(b, s // tm - first),
        in_specs=[pl.BlockSpec((1, tm, D_MODEL), lambda i, t: (i, t + first, 0)),
                  pl.BlockSpec((1, SUBLANES, D_MODEL), lambda i, t: (i, jnp.maximum((t + first) * per - 1, lo_blk), 0)),
                  pl.BlockSpec((1, SUBLANES, D_MODEL), lambda i, t: (i, jnp.minimum((t + first + 1) * per, hi_blk), 0)),
                  pl.BlockSpec((1, 8, D_MODEL), lambda i, t: (i, 0, 0)),
                  pl.BlockSpec((1, 8, D_MODEL), lambda i, t: (nb, 0, 0)),
                  _resident(wu.shape), _resident(wd.shape), _resident(cv.shape), _resident(vec.shape)],
        out_specs=pl.BlockSpec((1, tm, D_MODEL), lambda i, t: (i, t, 0)),
        out_shape=jax.ShapeDtypeStruct((b, out_rows, D_MODEL), F32),
        compiler_params=_params("parallel", "parallel"),
        name="conv_ffn_postnorm",
    )(h, h, h, mods, mods, wu, wd, cv, vec)


def _pad_rows(a, rows):
    return jnp.pad(a, ((0, rows - a.shape[0]),) + ((0, 0),) * (a.ndim - 1))


def _rope_tables(t_len, ctx):
    rows = t_len // GRID_W
    r, col = jnp.meshgrid(jnp.arange(rows, dtype=F32), jnp.arange(GRID_W, dtype=F32), indexing='ij')

    def tables(dim):
        quarter = dim // 4
        inv = ROPE_BASE ** (-jnp.arange(quarter, dtype=F32) / quarter)
        ang = jnp.concatenate([r.reshape(-1, 1) * inv, col.reshape(-1, 1) * inv], axis=-1)
        cos, sin = jnp.cos(ang), jnp.sin(ang)
        return jnp.concatenate([cos, cos], -1), jnp.concatenate([-sin, sin], -1)

    def with_ctx(cos, sin, width):
        cos = jnp.pad(cos, ((0, 0), (0, width - cos.shape[1])), constant_values=1.0)
        sin = jnp.pad(sin, ((0, 0), (0, width - sin.shape[1])))
        return (jnp.concatenate([jnp.ones((ctx, width), F32), cos], 0),
                jnp.concatenate([jnp.zeros((ctx, width), F32), sin], 0))

    return with_ctx(*tables(RET_HEAD_DIM), RET_HEAD_DIM), with_ctx(*tables(MLA_ROPE), LANES)


def _layer_params(l, p):
    w_in = p['w_in'][l]
    offs = np.cumsum((0, SEG_GATES, RET_WIDTH, RET_WIDTH, RET_WIDTH, RET_WIDTH, MLA_Q_RANK, MLA_KV_RANK, MLA_ROPE,
                      LRU_WIDTH, LRU_WIDTH, SSD_WIDTH, SSD_XBC, 2 * SSD_HEADS))
    piece = lambda i: w_in[:, offs[i]:offs[i + 1]]
    zeros = lambda n: jnp.zeros((D_MODEL, n), w_in.dtype)
    w_cat = jnp.concatenate([piece(0), piece(1), piece(2), piece(3), piece(4), piece(11), piece(10), piece(8), piece(9),
                             piece(5), piece(6), piece(7), zeros(SEG_MLA - MLA_Q_RANK - MLA_KV_RANK - MLA_ROPE)], axis=1)
    dt_w = piece(12).reshape(D_MODEL, 2, SSD_GROUPS, SSD_HG).transpose(0, 2, 1, 3).reshape(D_MODEL, SSD_GROUPS, 2 * SSD_HG)
    w_dt = jnp.pad(dt_w, ((0, 0), (0, 0), (0, LANES - 2 * SSD_HG))).reshape(D_MODEL, DT_COLS)

    def per_group(v):
        v = v.reshape(2, SSD_GROUPS, SSD_HG).transpose(1, 0, 2).reshape(SSD_GROUPS, 2 * SSD_HG)
        return jnp.pad(v, ((0, 0), (0, LANES - 2 * SSD_HG)))

    ssd_hp = jnp.stack([per_group(p['ssd_dt_bias'][l]), per_group(-jnp.exp(p['ssd_a_log'][l].astype(F32)))], axis=1)
    ssd_hp = jnp.pad(ssd_hp, ((0, 0), (0, 6), (0, 0)))

    wq = p['mla_w_uq'][l].reshape(MLA_Q_RANK, MLA_HEADS, MLA_NOPE + MLA_ROPE)
    wq = jnp.pad(wq, ((0, 0), (0, 0), (0, MLA_QK_PAD - MLA_NOPE - MLA_ROPE))).reshape(MLA_Q_RANK, MLA_HEADS * MLA_QK_PAD)

    gw = p['lru_gate_w'][l].reshape(4, LRU_WIDTH // LANES, 2, LRU_BLOCK, LRU_BLOCK)
    w_bd = jnp.einsum('ajpcd,pq->japcqd', gw, jnp.eye(2, dtype=gw.dtype)).reshape(LRU_WIDTH // LANES, 4, LANES, LANES)
    w_bd = w_bd.transpose(0, 2, 1, 3).reshape(LRU_WIDTH // LANES, LANES, 4 * LANES)
    lru_vec = jnp.concatenate([p['lru_conv_w'][l], p['lru_conv_b'][l][None], p['lru_gate_b'][l].reshape(4, LRU_WIDTH),
                               jax.nn.softplus(-p['lru_lambda'][l].astype(F32))], axis=0)

    log_g = jax.nn.log_sigmoid(p['ret_decay'][l].astype(F32))
    ret_lg = jnp.broadcast_to(_pad_rows(log_g, 8).T[:, :, None], (RET_HEADS, 8, RET_HEAD_DIM))

    return dict(
        ada_w=p['ada_w'][l].astype(BF16), ada_b=p['ada_b'][l][None],
        w_cat=w_cat.astype(BF16), w_dt=w_dt.astype(BF16),
        ret_lg=ret_lg, ret_gw=p['ret_gn_w'][l].reshape(RET_HEADS, 1, RET_HEAD_DIM),
        ret_gb=p['ret_gn_b'][l].reshape(RET_HEADS, 1, RET_HEAD_DIM),
        mla_qn=p['mla_q_norm'][l][None], mla_kn=p['mla_kv_norm'][l][None],
        mla_wq=wq.astype(BF16), mla_wkv=p['mla_w_ukv'][l].astype(BF16),
        lru_w=w_bd.astype(BF16), lru_vec=_pad_rows(lru_vec, 16),
        ssd_cv=_pad_rows(jnp.concatenate([p['ssd_conv_w'][l], p['ssd_conv_b'][l][None]], 0), 8),
        ssd_hp=ssd_hp, ssd_dsk=jnp.repeat(p['ssd_d'][l].astype(F32), SSD_HEAD_DIM)[None],
        ssd_nw=p['ssd_norm_w'][l][None],
        w_branch=p['w_branch'][l].astype(BF16), w_out=p['w_out'][l].astype(BF16),
        ln1=_pad_rows(jnp.stack([p['ln1_w'][l], p['ln1_b'][l]]), 8),
        ffn_wu=p['ffn_w_up'][l].astype(BF16), ffn_wd=p['ffn_w_down'][l].astype(BF16),
        ffn_cv=_pad_rows(jnp.concatenate([p['ffn_conv_w'][l], p['ffn_conv_b'][l][None]], 0), 8),
        ln2=_pad_rows(jnp.stack([p['ln2_w'][l], p['ln2_b'][l]]), 8),
    )


def kernel(x, c, ctx, c_ctx, ada_w, ada_b, w_in, ret_decay, ret_gn_w, ret_gn_b, mla_q_norm, mla_w_uq, mla_kv_norm, mla_w_ukv, lru_conv_w, lru_conv_b, lru_gate_w, lru_gate_b, lru_lambda, ssd_conv_w, ssd_conv_b, ssd_dt_bias, ssd_a_log, ssd_d, ssd_norm_w, w_branch, w_out, ln1_w, ln1_b, ffn_w_up, ffn_conv_w, ffn_conv_b, ffn_w_down, ln2_w, ln2_b):
    p = dict(ada_w=ada_w, ada_b=ada_b, w_in=w_in, ret_decay=ret_decay, ret_gn_w=ret_gn_w, ret_gn_b=ret_gn_b,
             mla_q_norm=mla_q_norm, mla_w_uq=mla_w_uq, mla_kv_norm=mla_kv_norm, mla_w_ukv=mla_w_ukv,
             lru_conv_w=lru_conv_w, lru_conv_b=lru_conv_b, lru_gate_w=lru_gate_w, lru_gate_b=lru_gate_b,
             lru_lambda=lru_lambda, ssd_conv_w=ssd_conv_w, ssd_conv_b=ssd_conv_b, ssd_dt_bias=ssd_dt_bias,
             ssd_a_log=ssd_a_log, ssd_d=ssd_d, ssd_norm_w=ssd_norm_w, w_branch=w_branch, w_out=w_out,
             ln1_w=ln1_w, ln1_b=ln1_b, ffn_w_up=ffn_w_up, ffn_conv_w=ffn_conv_w, ffn_conv_b=ffn_conv_b,
             ffn_w_down=ffn_w_down, ln2_w=ln2_w, ln2_b=ln2_b)
    batch, t_len, _ = x.shape
    n_ctx = ctx.shape[1]
    depth = ada_w.shape[0]
    assert n_ctx % ROW_TILE == 0 and t_len % ROW_TILE == 0 and t_len % GRID_W == 0
    alpha = (2 * depth) ** 0.25
    (ret_cos, ret_sin), (mla_cos, mla_sin) = _rope_tables(t_len, n_ctx)
    cond = _pad_rows(jnp.concatenate([c, c_ctx[None]], axis=0), -(-(batch + 1) // 8) * 8)
    h = jnp.concatenate([ctx, x], axis=1)
    for l in range(depth):
        with_ctx = l < depth - 1
        w = _layer_params(l, p)
        mods = _mod_call(cond, w['ada_w'], w['ada_b'])[:batch + 1].reshape(batch + 1, 6, D_MODEL)
        mods = jnp.pad(mods, ((0, 0), (0, 2), (0, 0)))
        p_gates, p_ret, p_ssd, p_lru, p_mla, p_dt = _inproj_call(h, mods, w['w_cat'], w['w_dt'], n_ctx)
        y_ret = _ret_call(p_ret, ret_cos, ret_sin, w['ret_lg'], w['ret_gw'], w['ret_gb'], n_ctx)
        q, k, v = _mla_prep_call(p_mla, w['mla_qn'], w['mla_kn'], w['mla_wq'], w['mla_wkv'], mla_cos, mla_sin)
        y_mla = _attn_call(q, k, v, n_ctx, with_ctx)
        y_lru = _lru_call(p_lru, w['lru_w'], w['lru_vec'], n_ctx)
        y_ssd = _ssd_call(p_ssd, p_dt, w['ssd_cv'], w['ssd_hp'], w['ssd_dsk'], n_ctx)
        h1 = _merge_call(p_gates, y_ret, y_mla, y_lru, y_ssd, h, mods, w['w_branch'], w['w_out'], w['ln1'], w['ssd_nw'],
                         n_ctx, with_ctx, alpha)
        h = _ffn_call(h1, mods, w['ffn_wu'], w['ffn_wd'], w['ffn_cv'], w['ln2'], n_ctx, with_ctx, alpha)
    return h
```

```python
import functools
import math

import jax
import jax.numpy as jnp
import numpy as np
from jax import lax
from jax.experimental import pallas as pl
from jax.experimental.pallas import tpu as pltpu

F32 = jnp.float32
BF16 = jnp.bfloat16

D_MODEL = 1024
GRID_W = 64
ROPE_BASE = 10000.0
LN_EPS = 1e-6
RMS_EPS = 1e-6

RET_HEADS = 4
RET_HEAD_DIM = 128
RET_WIDTH = RET_HEADS * RET_HEAD_DIM

MLA_HEADS = 4
MLA_Q_RANK = 384
MLA_KV_RANK = 256
MLA_NOPE = 128
MLA_ROPE = 64
MLA_V = 128
MLA_WIDTH = MLA_HEADS * MLA_V
MLA_QK_PAD = 256

LRU_WIDTH = 512
LRU_BLOCKS = 8
LRU_BLOCK = LRU_WIDTH // LRU_BLOCKS
LRU_C = 8.0

SSD_HEADS = 8
SSD_HEAD_DIM = 64
SSD_WIDTH = SSD_HEADS * SSD_HEAD_DIM
SSD_GROUPS = 2
SSD_STATE = 128
SSD_XBC = SSD_WIDTH + 2 * SSD_GROUPS * SSD_STATE
SSD_HG = SSD_HEADS // SSD_GROUPS

N_BRANCH = 4
D_FF = 2816

CHUNK = 128
LANES = 128
SUBLANES = 8
ROW_TILE = 256
N_CHUNK = 512
FF_CHUNK = 256
VMEM_LIMIT = 56 * 1024 * 1024

SEG_GATES = N_BRANCH * D_MODEL
SEG_RET = 4 * RET_WIDTH
SEG_SSD = SSD_XBC + SSD_WIDTH
SEG_LRU = 2 * LRU_WIDTH
SEG_MLA = 768
SEGS = (SEG_GATES, SEG_RET, SEG_SSD, SEG_LRU, SEG_MLA)
DT_COLS = SSD_GROUPS * LANES


def _dot(a, b):
    return jnp.dot(a, b, preferred_element_type=F32)


def _dot_nt(a, b):
    return lax.dot_general(a, b, (((1,), (1,)), ((), ())), preferred_element_type=F32)


def _split3(x):
    x1 = x.astype(BF16)
    r1 = x - x1.astype(F32)
    x2 = r1.astype(BF16)
    x3 = (r1 - x2.astype(F32)).astype(BF16)
    return x1, x2, x3


def _dot3_l(m, x):
    x1, x2, x3 = _split3(x)
    return _dot(m, x1) + _dot(m, x2) + _dot(m, x3)


def _dot3_r(x, m):
    x1, x2, x3 = _split3(x)
    return _dot(x1, m) + _dot(x2, m) + _dot(x3, m)


def _ln(x):
    mu = jnp.mean(x, axis=-1, keepdims=True)
    xc = x - mu
    var = jnp.mean(xc * xc, axis=-1, keepdims=True)
    return xc * lax.rsqrt(var + LN_EPS)


def _rms(x):
    return x * lax.rsqrt(jnp.mean(x * x, axis=-1, keepdims=True) + RMS_EPS)


def _sigmoid(x):
    return 0.5 * jnp.tanh(0.5 * x) + 0.5


def _silu(x):
    return x * _sigmoid(x)


def _softplus(x):
    return jnp.maximum(x, 0.0) + jnp.log1p(jnp.exp(-jnp.abs(x)))


def _gelu_tanh(x):
    return 0.5 * x * (1.0 + jnp.tanh(math.sqrt(2.0 / math.pi) * (x + 0.044715 * (x * x * x))))


def _iota(shape, dim):
    return lax.broadcasted_iota(jnp.int32, shape, dim)


def _resident(shape):
    nd = len(shape)
    return pl.BlockSpec(shape, lambda *_: (0,) * nd, pipeline_mode=pl.Buffered(1))


def _params(*sem):
    return pltpu.CompilerParams(dimension_semantics=sem, vmem_limit_bytes=VMEM_LIMIT)


def _mod_kernel(c_ref, w_ref, b_ref, o_ref):
    s = _silu(c_ref[...])
    o_ref[...] = _dot(s.astype(BF16), w_ref[...]) + b_ref[...]


def _mod_call(cc, w, b):
    rows = cc.shape[0]
    n = w.shape[1]
    return pl.pallas_call(
        _mod_kernel,
        grid=(n // D_MODEL,),
        in_specs=[
            pl.BlockSpec((rows, D_MODEL), lambda j: (0, 0)),
            pl.BlockSpec((D_MODEL, D_MODEL), lambda j: (0, j)),
            pl.BlockSpec((1, D_MODEL), lambda j: (0, j)),
        ],
        out_specs=pl.BlockSpec((rows, D_MODEL), lambda j: (0, j)),
        out_shape=jax.ShapeDtypeStruct((rows, n), F32),
        compiler_params=_params("arbitrary"),
        name="adaln_mod",
    )(cc, w, b)


def _pick_mod(is_ctx, mx_ref, mc_ref, row):
    return jnp.where(is_ctx, mc_ref[0, row:row + 1, :], mx_ref[0, row:row + 1, :])


def _inproj_kernel(h_ref, mx_ref, mc_ref, w_ref, wdt_ref, og, orr, os_, ol, om, odt, *, ctx):
    tm = h_ref.shape[1]
    is_ctx = pl.program_id(1) * tm < ctx
    shift = _pick_mod(is_ctx, mx_ref, mc_ref, 0)
    scale = _pick_mod(is_ctx, mx_ref, mc_ref, 1)
    xm = (_ln(h_ref[0]) * (1.0 + scale) + shift).astype(BF16)
    col = 0
    for oref, width in zip((og, orr, os_, ol, om), SEGS):
        for c0 in range(0, width, N_CHUNK):
            cw = min(N_CHUNK, width - c0)
            oref[0, :, c0:c0 + cw] = _dot(xm, w_ref[:, col + c0:col + c0 + cw]).astype(oref.dtype)
        col += width
    odt[0] = _dot(xm, wdt_ref[...])


def _inproj_call(h, mods, w_cat, w_dt, ctx):
    b, s, _ = h.shape
    tm = ROW_TILE
    nb = mods.shape[0] - 1
    tok = lambda w: pl.BlockSpec((1, tm, w), lambda i, t: (i, t, 0))
    outs = [jax.ShapeDtypeStruct((b, s, w), BF16) for w in SEGS] + [jax.ShapeDtypeStruct((b, s, DT_COLS), F32)]
    return pl.pallas_call(
        functools.partial(_inproj_kernel, ctx=ctx),
        grid=(b, s // tm),
        in_specs=[
            tok(D_MODEL),
            pl.BlockSpec((1, 8, D_MODEL), lambda i, t: (i, 0, 0)),
            pl.BlockSpec((1, 8, D_MODEL), lambda i, t: (nb, 0, 0)),
            _resident(w_cat.shape),
            _resident(w_dt.shape),
        ],
        out_specs=[tok(w) for w in SEGS] + [tok(DT_COLS)],
        out_shape=outs,
        compiler_params=_params("parallel", "parallel"),
        name="in_proj",
    )(h, mods, mods, w_cat, w_dt)


def _ret_kernel(q_ref, k_ref, v_ref, g_ref, cos_ref, sin_ref, lg_ref, gw_ref, gb_ref, o_ref,
                y_scr, q_scr, kv_scr, st_scr, *, ctx):
    s_len = q_ref.shape[1]
    nck, nctx = s_len // CHUNK, ctx // CHUNK
    hd = RET_HEAD_DIM
    scale = hd ** -0.5
    lgf = lg_ref[0, 0:1, :]
    lgb = lg_ref[0, 1:2, :]
    d = (_iota((CHUNK, CHUNK), 0) - _iota((CHUNK, CHUNK), 1)).astype(F32)
    dec = (jnp.where(d >= 0, jnp.exp(jnp.maximum(d, 0.0) * lgf), 0.0)
           + jnp.where(d <= 0, jnp.exp(jnp.maximum(-d, 0.0) * lgb), 0.0))
    pos = _iota((CHUNK, 1), 0).astype(F32)
    posl = _iota((1, CHUNK), 1).astype(F32)
    qdec_f = jnp.exp((pos + 1.0) * lgf)
    qdec_b = jnp.exp((CHUNK - pos) * lgb)
    kdec_f = jnp.exp((CHUNK - 1.0 - posl) * lgf)
    kdec_b = jnp.exp(posl * lgb)
    cdec_f = jnp.exp(CHUNK * lgf)
    cdec_b = jnp.exp(CHUNK * lgb)
    gw = gw_ref[0]
    gb = gb_ref[0]

    def rows(n):
        return pl.ds(pl.multiple_of(n * CHUNK, CHUNK), CHUNK)

    def intra(n, carry):
        r = rows(n)
        cos = cos_ref[r, :]
        sin = sin_ref[r, :]
        q = q_ref[0, r, :].astype(F32)
        k = k_ref[0, r, :].astype(F32)
        v = v_ref[0, r, :]
        qb = (q * cos + pltpu.roll(q, hd // 2, 1) * sin).astype(BF16)
        kr = (k * cos + pltpu.roll(k, hd // 2, 1) * sin) * scale
        q_scr[r, :] = qb
        sc = _dot_nt(qb, kr.astype(BF16)) * dec
        y_scr[r, :] = _dot(sc.astype(BF16), v)
        kt = kr.T
        kv_scr[n] = _dot(jnp.concatenate([kt * kdec_f, kt * kdec_b], axis=0).astype(BF16), v)
        return carry

    lax.fori_loop(0, nck, intra, 0, unroll=3)

    def states(i, sts):
        sf, sb = sts
        st_scr[i, 0:hd, :] = sf.astype(BF16)
        sf = sf * cdec_f + kv_scr[i, 0:hd, :]
        j = jnp.where(i < nctx, nctx - 1 - i, nck - 1 - (i - nctx))
        st_scr[j, hd:2 * hd, :] = sb.astype(BF16)
        sb = sb * cdec_b + kv_scr[j, hd:2 * hd, :]
        return sf, sb

    zero = jnp.zeros((hd, hd), F32)
    lax.fori_loop(0, nck, states, (zero, zero))

    def inter(n, carry):
        r = rows(n)
        q = q_scr[r, :].astype(F32)
        qq = jnp.concatenate([q * qdec_f, q * qdec_b], axis=1).astype(BF16)
        y = y_scr[r, :] + _dot(qq, st_scr[n])
        g = g_ref[0, r, :].astype(F32)
        o_ref[0, r, :] = (_silu(g) * (_ln(y) * gw + gb)).astype(o_ref.dtype)
        return carry

    lax.fori_loop(0, nck, inter, 0, unroll=3)


def _ret_call(p_ret, cos, sin, lg, gw, gb, ctx):
    b, s, _ = p_ret.shape
    col = lambda off: pl.BlockSpec((1, s, RET_HEAD_DIM), lambda i, h: (i, 0, off + h))
    vec = pl.BlockSpec((1, 1, RET_HEAD_DIM), lambda i, h: (h, 0, 0))
    tab = pl.BlockSpec((s, RET_HEAD_DIM), lambda i, h: (0, 0))
    return pl.pallas_call(
        functools.partial(_ret_kernel, ctx=ctx),
        grid=(b, RET_HEADS),
        in_specs=[col(0), col(RET_HEADS), col(2 * RET_HEADS), col(3 * RET_HEADS), tab, tab,
                  pl.BlockSpec((1, 8, RET_HEAD_DIM), lambda i, h: (h, 0, 0)), vec, vec],
        out_specs=pl.BlockSpec((1, s, RET_HEAD_DIM), lambda i, h: (i, 0, h)),
        out_shape=jax.ShapeDtypeStruct((b, s, RET_WIDTH), BF16),
        scratch_shapes=[pltpu.VMEM((s, RET_HEAD_DIM), F32), pltpu.VMEM((s, RET_HEAD_DIM), BF16),
                        pltpu.VMEM((s // CHUNK, 2 * RET_HEAD_DIM, RET_HEAD_DIM), F32),
                        pltpu.VMEM((s // CHUNK, 2 * RET_HEAD_DIM, RET_HEAD_DIM), BF16)],
        compiler_params=_params("parallel", "parallel"),
        name="retention",
    )(p_ret, p_ret, p_ret, p_ret, cos, sin, lg, gw, gb)


def _mla_prep_kernel(m_ref, qn_ref, kn_ref, wq_ref, wkv_ref, cos_ref, sin_ref, q_ref, k_ref, v_ref):
    m = m_ref[0].astype(F32)
    cq = _rms(m[:, :MLA_Q_RANK]) * qn_ref[...]
    ckv = _rms(m[:, MLA_Q_RANK:MLA_Q_RANK + MLA_KV_RANK]) * kn_ref[...]
    kr = m[:, MLA_Q_RANK + MLA_KV_RANK:]
    q = _dot(cq.astype(BF16), wq_ref[...])
    kv = _dot(ckv.astype(BF16), wkv_ref[...])
    cos = cos_ref[...]
    sin = sin_ref[...]
    first_half = (_iota(cos.shape, 1) % MLA_ROPE) < (MLA_ROPE // 2)

    def rope(x):
        swapped = jnp.where(first_half, pltpu.roll(x, LANES - MLA_ROPE // 2, 1), pltpu.roll(x, MLA_ROPE // 2, 1))
        return x * cos + swapped * sin

    scale = (MLA_NOPE + MLA_ROPE) ** -0.5
    kr = rope(kr).astype(BF16)
    for h in range(MLA_HEADS):
        c0 = h * MLA_QK_PAD
        q_ref[0, h, :, :MLA_NOPE] = (q[:, c0:c0 + MLA_NOPE] * scale).astype(BF16)
        q_ref[0, h, :, MLA_NOPE:] = (rope(q[:, c0 + MLA_NOPE:c0 + MLA_QK_PAD]) * scale).astype(BF16)
        k_ref[0, h, :, :MLA_NOPE] = kv[:, c0:c0 + MLA_NOPE].astype(BF16)
        k_ref[0, h, :, MLA_NOPE:] = kr
        v_ref[0, h] = kv[:, c0 + MLA_NOPE:c0 + MLA_NOPE + MLA_V].astype(BF16)


def _mla_prep_call(p_mla, qn, kn, wq, wkv, cos, sin):
    b, s, _ = p_mla.shape
    tm = ROW_TILE
    head = lambda w: pl.BlockSpec((1, MLA_HEADS, tm, w), lambda i, t: (i, 0, t, 0))
    tab = pl.BlockSpec((tm, LANES), lambda i, t: (t, 0))
    return pl.pallas_call(
        _mla_prep_kernel,
        grid=(b, s // tm),
        in_specs=[pl.BlockSpec((1, tm, SEG_MLA), lambda i, t: (i, t, 0)),
                  _resident(qn.shape), _resident(kn.shape), _resident(wq.shape), _resident(wkv.shape), tab, tab],
        out_specs=[head(MLA_QK_PAD), head(MLA_QK_PAD), head(MLA_V)],
        out_shape=[jax.ShapeDtypeStruct((b, MLA_HEADS, s, MLA_QK_PAD), BF16),
                   jax.ShapeDtypeStruct((b, MLA_HEADS, s, MLA_QK_PAD), BF16),
                   jax.ShapeDtypeStruct((b, MLA_HEADS, s, MLA_V), BF16)],
        compiler_params=_params("parallel", "parallel"),
        name="mla_prep",
    )(p_mla, qn, kn, wq, wkv, cos, sin)


def _attn_kernel(q_ref, k_ref, v_ref, o_ref, *, ctx, first_tile):
    tq = q_ref.shape[2]
    s_len = k_ref.shape[2]
    q = q_ref[0, 0]
    row0 = (pl.program_id(2) + first_tile) * tq

    def attend(nk):
        sc = _dot_nt(q, k_ref[0, 0, :nk, :])
        p = jnp.exp(sc - jnp.max(sc, axis=-1, keepdims=True))
        o = _dot(p.astype(BF16), v_ref[0, 0, :nk, :]) / jnp.sum(p, axis=-1, keepdims=True)
        o_ref[0] = o.astype(o_ref.dtype)

    pl.when(row0 < ctx)(lambda: attend(ctx))
    pl.when(row0 >= ctx)(lambda: attend(s_len))


def _attn_call(q, k, v, ctx, with_ctx):
    b, nh, s, _ = q.shape
    tq = ROW_TILE
    first = 0 if with_ctx else ctx // tq
    kv = lambda w: pl.BlockSpec((1, 1, s, w), lambda i, h, t: (i, h, 0, 0))
    return pl.pallas_call(
        functools.partial(_attn_kernel, ctx=ctx, first_tile=first),
        grid=(b, nh, s // tq - first),
        in_specs=[pl.BlockSpec((1, 1, tq, MLA_QK_PAD), lambda i, h, t: (i, h, t + first, 0)),
                  kv(MLA_QK_PAD), kv(MLA_V)],
        out_specs=pl.BlockSpec((1, tq, MLA_V), lambda i, h, t: (i, t + first, h)),
        out_shape=jax.ShapeDtypeStruct((b, s, MLA_WIDTH), BF16),
        compiler_params=_params("parallel", "parallel", "arbitrary"),
        name="mla_attention",
    )(q, k, v)


def _padded_row(n, nctx):
    return pl.multiple_of(n * CHUNK + jnp.where(n < nctx, SUBLANES, 2 * SUBLANES), SUBLANES)


def _fill_padded(src_ref, col, pad_scr, *, ctx):
    s_len = src_ref.shape[1]
    width = pad_scr.shape[1]
    nctx = ctx // CHUNK
    zeros = jnp.zeros((SUBLANES, width), F32)
    pad_scr[0:SUBLANES, :] = zeros
    pad_scr[ctx + SUBLANES:ctx + 2 * SUBLANES, :] = zeros
    pad_scr[s_len + 2 * SUBLANES:s_len + 3 * SUBLANES, :] = zeros

    def body(n, carry):
        r = pl.ds(pl.multiple_of(n * CHUNK, CHUNK), CHUNK)
        pad_scr[pl.ds(_padded_row(n, nctx), CHUNK), :] = src_ref[0, r, col].astype(F32)
        return carry

    lax.fori_loop(0, s_len // CHUNK, body, 0)


def _conv4(pad_scr, n, nctx, taps_ref):
    base = _padded_row(n, nctx)
    win = pad_scr[pl.ds(base - SUBLANES, CHUNK + 2 * SUBLANES), :]
    rows = CHUNK + 2 * SUBLANES
    lo, hi = SUBLANES, SUBLANES + CHUNK
    acc = taps_ref[4:5, :] + win[lo:hi] * taps_ref[1:2, :]
    acc = acc + pltpu.roll(win, 1, 0)[lo:hi] * taps_ref[0:1, :]
    acc = acc + pltpu.roll(win, rows - 1, 0)[lo:hi] * taps_ref[2:3, :]
    acc = acc + pltpu.roll(win, rows - 2, 0)[lo:hi] * taps_ref[3:4, :]
    return acc


def _lru_kernel(x_ref, g_ref, w_ref, vec_ref, o_ref, pad_scr, af_scr, bf_scr, ab_scr, bb_scr, *, ctx):
    s_len = x_ref.shape[1]
    nck, nctx = s_len // CHUNK, ctx // CHUNK
    ntile, nctile = s_len // SUBLANES, ctx // SUBLANES
    _fill_padded(x_ref, slice(None), pad_scr, ctx=ctx)
    rt = _iota((CHUNK, LANES), 0) % SUBLANES

    def gates(n, carry):
        r = pl.ds(pl.multiple_of(n * CHUNK, CHUNK), CHUNK)
        u = _conv4(pad_scr, n, nctx, vec_ref)
        z = _dot(u.astype(BF16), w_ref[0])
        for d, (a_scr, b_scr) in enumerate(((af_scr, bf_scr), (ab_scr, bb_scr))):
            rg = _sigmoid(z[:, (2 * d) * LANES:(2 * d + 1) * LANES] + vec_ref[5 + 2 * d:6 + 2 * d, :])
            ig = _sigmoid(z[:, (2 * d + 1) * LANES:(2 * d + 2) * LANES] + vec_ref[6 + 2 * d:7 + 2 * d, :])
            log_a = -LRU_C * rg * vec_ref[9 + d:10 + d, :]
            a = jnp.exp(log_a)
            bv = jnp.sqrt(-jnp.tanh(log_a) * (1.0 + a * a)) * (ig * u)
            for sh in (1, 2, 4):
                if d == 0:
                    keep = rt >= sh
                    a_s, b_s = pltpu.roll(a, sh, 0), pltpu.roll(bv, sh, 0)
                else:
                    keep = rt < SUBLANES - sh
                    a_s, b_s = pltpu.roll(a, CHUNK - sh, 0), pltpu.roll(bv, CHUNK - sh, 0)
                bv = bv + a * jnp.where(keep, b_s, 0.0)
                a = a * jnp.where(keep, a_s, 1.0)
            a_scr[r, :] = a
            b_scr[r, :] = bv
        return carry

    lax.fori_loop(0, nck, gates, 0)

    def carry_step(i, hs):
        hf, hb = hs
        rf = pl.ds(pl.multiple_of(i * SUBLANES, SUBLANES), SUBLANES)
        tf = af_scr[rf, :] * hf + bf_scr[rf, :]
        bf_scr[rf, :] = tf
        j = jnp.where(i < nctile, nctile - 1 - i, ntile - 1 - (i - nctile))
        rb = pl.ds(pl.multiple_of(j * SUBLANES, SUBLANES), SUBLANES)
        tb = ab_scr[rb, :] * hb + bb_scr[rb, :]
        bb_scr[rb, :] = tb
        return (jnp.broadcast_to(tf[SUBLANES - 1:SUBLANES, :], (SUBLANES, LANES)),
                jnp.broadcast_to(tb[0:1, :], (SUBLANES, LANES)))

    zero = jnp.zeros((SUBLANES, LANES), F32)
    lax.fori_loop(0, ntile, carry_step, (zero, zero))

    def finish(n, carry):
        r = pl.ds(pl.multiple_of(n * CHUNK, CHUNK), CHUNK)
        g = g_ref[0, r, :].astype(F32)
        o_ref[0, r, :] = ((bf_scr[r, :] + bb_scr[r, :]) * _gelu_tanh(g)).astype(o_ref.dtype)
        return carry

    lax.fori_loop(0, nck, finish, 0)


def _lru_call(p_lru, w_bd, vec, ctx):
    b, s, _ = p_lru.shape
    ncb = LRU_WIDTH // LANES
    return pl.pallas_call(
        functools.partial(_lru_kernel, ctx=ctx),
        grid=(b, ncb),
        in_specs=[pl.BlockSpec((1, s, LANES), lambda i, j: (i, 0, j)),
                  pl.BlockSpec((1, s, LANES), lambda i, j: (i, 0, ncb + j)),
                  pl.BlockSpec((1, LANES, 4 * LANES), lambda i, j: (j, 0, 0)),
                  pl.BlockSpec((16, LANES), lambda i, j: (0, j))],
        out_specs=pl.BlockSpec((1, s, LANES), lambda i, j: (i, 0, j)),
        out_shape=jax.ShapeDtypeStruct((b, s, LRU_WIDTH), BF16),
        scratch_shapes=[pltpu.VMEM((s + 3 * SUBLANES, LANES), F32)] + [pltpu.VMEM((s, LANES), F32)] * 4,
        compiler_params=_params("parallel", "parallel"),
        name="rg_lru",
    )(p_lru, p_lru, w_bd, vec)


def _expand_heads(m, lane0, rows):
    width = SSD_HG * SSD_HEAD_DIM
    lane = _iota((rows, width), 1)
    out = jnp.broadcast_to(m[:, lane0 + SSD_HG - 1:lane0 + SSD_HG], (rows, width))
    for h in range(SSD_HG - 2, -1, -1):
        out = jnp.where(lane < (h + 1) * SSD_HEAD_DIM, m[:, lane0 + h:lane0 + h + 1], out)
    return out


def _ssd_kernel(x_ref, b_ref, c_ref, z_ref, dt_ref, cvx_ref, cvb_ref, cvc_ref, hp_ref, dsk_ref, o_ref,
                padx, padb, padc, xs_scr, bs_scr, cs_scr, y_scr, mf_scr, mb_scr, ds_scr, cd_scr, st_scr, sf_scr, sb_scr,
                *, ctx):
    s_len = x_ref.shape[1]
    nck, nctx = s_len // CHUNK, ctx // CHUNK
    gw = SSD_HG * SSD_HEAD_DIM
    _fill_padded(x_ref, slice(None), padx, ctx=ctx)
    _fill_padded(b_ref, slice(None), padb, ctx=ctx)
    _fill_padded(c_ref, slice(None), padc, ctx=ctx)

    def conv(n, carry):
        r = pl.ds(pl.multiple_of(n * CHUNK, CHUNK), CHUNK)
        xs_scr[r, :] = _silu(_conv4(padx, n, nctx, cvx_ref))
        bs_scr[r, :] = _silu(_conv4(padb, n, nctx, cvb_ref))
        cs_scr[r, :] = _silu(_conv4(padc, n, nctx, cvc_ref))
        return carry

    lax.fori_loop(0, nck, conv, 0)

    ii = _iota((CHUNK, CHUNK), 0)
    jj = _iota((CHUNK, CHUNK), 1)
    tri_l = (jj <= ii).astype(BF16)
    spread = (_iota((LANES, 2 * gw), 0) == _iota((LANES, 2 * gw), 1) // SSD_HEAD_DIM).astype(BF16)

    def expand(m):
        hi = m.astype(BF16)
        lo = (m - hi.astype(F32)).astype(BF16)
        return _dot(hi, spread) + _dot(lo, spread)

    lane = _iota((CHUNK, LANES), 1)
    lane256 = _iota((CHUNK, gw), 1)
    is_fwd_lane = lane < SSD_HG
    dt_bias = hp_ref[0, 0:1, :]
    a_neg = hp_ref[0, 1:2, :]

    def rows(n):
        return pl.ds(pl.multiple_of(n * CHUNK, CHUNK), CHUNK)

    def intra(n, carry):
        r = rows(n)
        dt = _softplus(dt_ref[0, r, :] + dt_bias)
        dta = dt * a_neg
        af = _dot3_l(tri_l, dta)
        tot = af[CHUNK - 1:CHUNK, :]
        rv = tot - af + dta
        cum = jnp.where(is_fwd_lane, af, rv)
        cum_t = cum.T
        dt_t = dt.T
        xg = xs_scr[r, :]
        bsf = bs_scr[r, :]
        cg = cs_scr[r, :].astype(BF16)
        cb = _dot_nt(cg, bsf.astype(BF16))
        y = jnp.zeros((CHUNK, gw), F32)
        for h in range(SSD_HG):
            hb = SSD_HG + h
            arg = jnp.where(ii >= jj, cum[:, h:h + 1] - cum_t[h:h + 1, :], cum[:, hb:hb + 1] - cum_t[hb:hb + 1, :])
            wdt = jnp.where(ii > jj, dt_t[h:h + 1, :],
                            jnp.where(ii < jj, dt_t[hb:hb + 1, :], dt_t[h:h + 1, :] + dt_t[hb:hb + 1, :]))
            w = (cb * jnp.exp(arg) * wdt).astype(BF16)
            in_head = (lane256 >= h * SSD_HEAD_DIM) & (lane256 < (h + 1) * SSD_HEAD_DIM)
            y = y + _dot(w, jnp.where(in_head, xg, 0.0).astype(BF16))
        y_scr[r, :] = y
        ecum = jnp.exp(cum)
        mult = expand(ecum)
        mf_scr[r, :] = mult[:, 0:gw]
        mb_scr[r, :] = mult[:, gw:2 * gw]
        edge = jnp.where(is_fwd_lane[0:1], tot, rv[0:1, :])
        wgt = expand(jnp.exp(edge - cum) * dt)
        bt = bsf.T.astype(BF16)
        ds_scr[n, :, 0:gw] = _dot(bt, (xg * wgt[:, 0:gw]).astype(BF16))
        ds_scr[n, :, gw:2 * gw] = _dot(bt, (xg * wgt[:, gw:2 * gw]).astype(BF16))
        cd_scr[n] = jnp.broadcast_to(jnp.exp(edge), (SUBLANES, LANES))
        return carry

    lax.fori_loop(0, nck, intra, 0, unroll=2)

    sf_scr[...] = jnp.zeros(sf_scr.shape, F32)
    sb_scr[...] = jnp.zeros(sb_scr.shape, F32)

    def states(i, carry):
        sf = sf_scr[...]
        st_scr[i, :, 0:gw] = sf.astype(BF16)
        sf_scr[...] = sf * _expand_heads(cd_scr[i, 0:1, :], 0, 1) + ds_scr[i, :, 0:gw]
        j = jnp.where(i < nctx, nctx - 1 - i, nck - 1 - (i - nctx))
        sb = sb_scr[...]
        st_scr[j, :, gw:2 * gw] = sb.astype(BF16)
        sb_scr[...] = sb * _expand_heads(cd_scr[j, 0:1, :], SSD_HG, 1) + ds_scr[j, :, gw:2 * gw]
        return carry

    lax.fori_loop(0, nck, states, 0)

    def inter(n, carry):
        r = rows(n)
        ys = _dot(cs_scr[r, :].astype(BF16), st_scr[n])
        y = y_scr[r, :] + ys[:, 0:gw] * mf_scr[r, :] + ys[:, gw:2 * gw] * mb_scr[r, :] + dsk_ref[...] * xs_scr[r, :]
        o_ref[0, r, :] = (y * _silu(z_ref[0, r, :].astype(F32))).astype(o_ref.dtype)
        return carry

    lax.fori_loop(0, nck, inter, 0, unroll=2)


def _ssd_call(p_ssd, dt, cv, hp, dsk, ctx):
    b, s, _ = p_ssd.shape
    gw = SSD_HG * SSD_HEAD_DIM
    nx = SSD_WIDTH // LANES
    nbc = SSD_GROUPS * SSD_STATE // LANES
    pad = lambda w: pltpu.VMEM((s + 3 * SUBLANES, w), F32)
    return pl.pallas_call(
        functools.partial(_ssd_kernel, ctx=ctx),
        grid=(b, SSD_GROUPS),
        in_specs=[pl.BlockSpec((1, s, gw), lambda i, g: (i, 0, g)),
                  pl.BlockSpec((1, s, LANES), lambda i, g: (i, 0, nx + g)),
                  pl.BlockSpec((1, s, LANES), lambda i, g: (i, 0, nx + nbc + g)),
                  pl.BlockSpec((1, s, gw), lambda i, g: (i, 0, SSD_XBC // gw + g)),
                  pl.BlockSpec((1, s, LANES), lambda i, g: (i, 0, g)),
                  pl.BlockSpec((8, gw), lambda i, g: (0, g)),
                  pl.BlockSpec((8, LANES), lambda i, g: (0, nx + g)),
                  pl.BlockSpec((8, LANES), lambda i, g: (0, nx + nbc + g)),
                  pl.BlockSpec((1, 8, LANES), lambda i, g: (g, 0, 0)),
                  pl.BlockSpec((1, gw), lambda i, g: (0, g))],
        out_specs=pl.BlockSpec((1, s, gw), lambda i, g: (i, 0, g)),
        out_shape=jax.ShapeDtypeStruct((b, s, SSD_WIDTH), BF16),
        scratch_shapes=[pad(gw), pad(LANES), pad(LANES),
                        pltpu.VMEM((s, gw), F32), pltpu.VMEM((s, LANES), F32), pltpu.VMEM((s, LANES), F32),
                        pltpu.VMEM((s, gw), F32), pltpu.VMEM((s, gw), F32), pltpu.VMEM((s, gw), F32),
                        pltpu.VMEM((s // CHUNK, SSD_STATE, 2 * gw), F32), pltpu.VMEM((s // CHUNK, SUBLANES, LANES), F32),
                        pltpu.VMEM((s // CHUNK, SSD_STATE, 2 * gw), BF16),
                        pltpu.VMEM((SSD_STATE, gw), F32), pltpu.VMEM((SSD_STATE, gw), F32)],
        compiler_params=_params("parallel", "parallel"),
        name="ssd",
    )(p_ssd, p_ssd, p_ssd, p_ssd, dt, cv, cv, cv, hp, dsk)


def _merge_kernel(g_ref, ret_ref, mla_ref, lru_ref, ssd_ref, h_ref, mx_ref, mc_ref, wb_ref, wo_ref, vec_ref, nw_ref,
                  o_ref, *, ctx, first_tile, alpha):
    tm = h_ref.shape[1]
    is_ctx = (pl.program_id(1) + first_tile) * tm < ctx
    ssd = (_rms(ssd_ref[0].astype(F32)) * nw_ref[...]).astype(BF16)
    acc = None
    for i, br in enumerate((ret_ref[0], mla_ref[0], lru_ref[0], ssd)):
        gate = _sigmoid(g_ref[0, :, i * D_MODEL:(i + 1) * D_MODEL].astype(F32))
        term = gate * _dot(br, wb_ref[i])
        acc = term if acc is None else acc + term
    o = _dot(acc.astype(BF16), wo_ref[...])
    y = _ln(alpha * h_ref[0] + _pick_mod(is_ctx, mx_ref, mc_ref, 2) * o)
    o_ref[0] = y * vec_ref[0:1, :] + vec_ref[1:2, :]


def _merge_call(p_gates, y_ret, y_mla, y_lru, y_ssd, h, mods, wb, wo, vec, nw, ctx, with_ctx, alpha):
    b, s, _ = h.shape
    tm = ROW_TILE
    first = 0 if with_ctx else ctx // tm
    nb = mods.shape[0] - 1
    tok = lambda w: pl.BlockSpec((1, tm, w), lambda i, t: (i, t + first, 0))
    return pl.pallas_call(
        functools.partial(_merge_kernel, ctx=ctx, first_tile=first, alpha=alpha),
        grid=(b, s // tm - first),
        in_specs=[tok(SEG_GATES), tok(RET_WIDTH), tok(MLA_WIDTH), tok(LRU_WIDTH), tok(SSD_WIDTH), tok(D_MODEL),
                  pl.BlockSpec((1, 8, D_MODEL), lambda i, t: (i, 0, 0)),
                  pl.BlockSpec((1, 8, D_MODEL), lambda i, t: (nb, 0, 0)),
                  _resident(wb.shape), _resident(wo.shape), _resident(vec.shape), _resident(nw.shape)],
        out_specs=tok(D_MODEL),
        out_shape=jax.ShapeDtypeStruct((b, s, D_MODEL), F32),
        compiler_params=_params("parallel", "parallel"),
        name="merge_postnorm",
    )(p_gates, y_ret, y_mla, y_lru, y_ssd, h, mods, mods, wb, wo, vec, nw)


def _ffn_kernel(h_ref, hp_ref, hn_ref, mx_ref, mc_ref, wu_ref, wd_ref, cv_ref, vec_ref, o_ref, *,
                ctx, s_len, first_tile, alpha):
    tm = h_ref.shape[1]
    row0 = (pl.program_id(1) + first_tile) * tm
    is_ctx = row0 < ctx
    shift = _pick_mod(is_ctx, mx_ref, mc_ref, 3)
    scale = _pick_mod(is_ctx, mx_ref, mc_ref, 4)
    h = h_ref[0]
    prev_ok = jnp.logical_and(row0 != 0, row0 != ctx)
    next_ok = jnp.logical_and(row0 + tm != ctx, row0 + tm != s_len)
    mod = lambda x: _ln(x) * (1.0 + scale) + shift
    xm = jnp.concatenate([jnp.where(prev_ok, mod(hp_ref[0]), 0.0), mod(h), jnp.where(next_ok, mod(hn_ref[0]), 0.0)],
                         axis=0).astype(BF16)
    rows = tm + 2 * SUBLANES
    lo, hi = SUBLANES, SUBLANES + tm

    def conv(u, c0):
        out = cv_ref[3:4, c0:c0 + FF_CHUNK] + u[lo:hi] * cv_ref[1:2, c0:c0 + FF_CHUNK]
        out = out + pltpu.roll(u, 1, 0)[lo:hi] * cv_ref[0:1, c0:c0 + FF_CHUNK]
        return out + pltpu.roll(u, rows - 1, 0)[lo:hi] * cv_ref[2:3, c0:c0 + FF_CHUNK]

    acc = jnp.zeros((tm, D_MODEL), F32)
    for c0 in range(0, D_FF, FF_CHUNK):
        g = conv(_dot(xm, wu_ref[:, c0:c0 + FF_CHUNK]), c0)
        v = conv(_dot(xm, wu_ref[:, D_FF + c0:D_FF + c0 + FF_CHUNK]), D_FF + c0)
        acc = acc + _dot((_silu(g) * v).astype(BF16), wd_ref[c0:c0 + FF_CHUNK, :])
    y = _ln(alpha * h + _pick_mod(is_ctx, mx_ref, mc_ref, 5) * acc)
    o_ref[0] = y * vec_ref[0:1, :] + vec_ref[1:2, :]


def _ffn_call(h, mods, wu, wd, cv, vec, ctx, with_ctx, alpha):
    b, s, _ = h.shape
    tm = ROW_TILE
    first = 0 if with_ctx else ctx // tm
    nb = mods.shape[0] - 1
    per = tm // SUBLANES
    lo_blk = first * per
    hi_blk = s // SUBLANES - 1
    out_rows = s - first * tm
    return pl.pallas_call(
        functools.partial(_ffn_kernel, ctx=ctx, s_len=s, first_tile=first, alpha=alpha),
        grid=(b, s // tm - first),
        in_specs=[pl.BlockSpec((1, tm, D_MODEL), lambda i, t: (i, t + first, 0)),
                  pl.BlockSpec((1, SUBLANES, D_MODEL), lambda i, t: (i, jnp.maximum((t + first) * per - 1, lo_blk), 0)),
                  pl.BlockSpec((1, SUBLANES, D_MODEL), lambda i, t: (i, jnp.minimum((t + first + 1) * per, hi_blk), 0)),
                  pl.BlockSpec((1, 8, D_MODEL), lambda i, t: (i, 0, 0)),
                  pl.BlockSpec((1, 8, D_MODEL), lambda i, t: (nb, 0, 0)),
                  _resident(wu.shape), _resident(wd.shape), _resident(cv.shape), _resident(vec.shape)],
        out_specs=pl.BlockSpec((1, tm, D_MODEL), lambda i, t: (i, t, 0)),
        out_shape=jax.ShapeDtypeStruct((b, out_rows, D_MODEL), F32),
        compiler_params=_params("parallel", "parallel"),
        name="conv_ffn_postnorm",
    )(h, h, h, mods, mods, wu, wd, cv, vec)


def _pad_rows(a, rows):
    return jnp.pad(a, ((0, rows - a.shape[0]),) + ((0, 0),) * (a.ndim - 1))


def _rope_tables(t_len, ctx):
    rows = t_len // GRID_W
    r, col = jnp.meshgrid(jnp.arange(rows, dtype=F32), jnp.arange(GRID_W, dtype=F32), indexing='ij')

    def tables(dim):
        quarter = dim // 4
        inv = ROPE_BASE ** (-jnp.arange(quarter, dtype=F32) / quarter)
        ang = jnp.concatenate([r.reshape(-1, 1) * inv, col.reshape(-1, 1) * inv], axis=-1)
        cos, sin = jnp.cos(ang), jnp.sin(ang)
        return jnp.concatenate([cos, cos], -1), jnp.concatenate([-sin, sin], -1)

    def with_ctx(cos, sin, width):
        cos = jnp.pad(cos, ((0, 0), (0, width - cos.shape[1])), constant_values=1.0)
        sin = jnp.pad(sin, ((0, 0), (0, width - sin.shape[1])))
        return (jnp.concatenate([jnp.ones((ctx, width), F32), cos], 0),
                jnp.concatenate([jnp.zeros((ctx, width), F32), sin], 0))

    return with_ctx(*tables(RET_HEAD_DIM), RET_HEAD_DIM), with_ctx(*tables(MLA_ROPE), LANES)


def _layer_params(l, p):
    w_in = p['w_in'][l]
    offs = np.cumsum((0, SEG_GATES, RET_WIDTH, RET_WIDTH, RET_WIDTH, RET_WIDTH, MLA_Q_RANK, MLA_KV_RANK, MLA_ROPE,
                      LRU_WIDTH, LRU_WIDTH, SSD_WIDTH, SSD_XBC, 2 * SSD_HEADS))
    piece = lambda i: w_in[:, offs[i]:offs[i + 1]]
    zeros = lambda n: jnp.zeros((D_MODEL, n), w_in.dtype)
    w_cat = jnp.concatenate([piece(0), piece(1), piece(2), piece(3), piece(4), piece(11), piece(10), piece(8), piece(9),
                             piece(5), piece(6), piece(7), zeros(SEG_MLA - MLA_Q_RANK - MLA_KV_RANK - MLA_ROPE)], axis=1)
    dt_w = piece(12).reshape(D_MODEL, 2, SSD_GROUPS, SSD_HG).transpose(0, 2, 1, 3).reshape(D_MODEL, SSD_GROUPS, 2 * SSD_HG)
    w_dt = jnp.pad(dt_w, ((0, 0), (0, 0), (0, LANES - 2 * SSD_HG))).reshape(D_MODEL, DT_COLS)

    def per_group(v):
        v = v.reshape(2, SSD_GROUPS, SSD_HG).transpose(1, 0, 2).reshape(SSD_GROUPS, 2 * SSD_HG)
        return jnp.pad(v, ((0, 0), (0, LANES - 2 * SSD_HG)))

    ssd_hp = jnp.stack([per_group(p['ssd_dt_bias'][l]), per_group(-jnp.exp(p['ssd_a_log'][l].astype(F32)))], axis=1)
    ssd_hp = jnp.pad(ssd_hp, ((0, 0), (0, 6), (0, 0)))

    wq = p['mla_w_uq'][l].reshape(MLA_Q_RANK, MLA_HEADS, MLA_NOPE + MLA_ROPE)
    wq = jnp.pad(wq, ((0, 0), (0, 0), (0, MLA_QK_PAD - MLA_NOPE - MLA_ROPE))).reshape(MLA_Q_RANK, MLA_HEADS * MLA_QK_PAD)

    gw = p['lru_gate_w'][l].reshape(4, LRU_WIDTH // LANES, 2, LRU_BLOCK, LRU_BLOCK)
    w_bd = jnp.einsum('ajpcd,pq->japcqd', gw, jnp.eye(2, dtype=gw.dtype)).reshape(LRU_WIDTH // LANES, 4, LANES, LANES)
    w_bd = w_bd.transpose(0, 2, 1, 3).reshape(LRU_WIDTH // LANES, LANES, 4 * LANES)
    lru_vec = jnp.concatenate([p['lru_conv_w'][l], p['lru_conv_b'][l][None], p['lru_gate_b'][l].reshape(4, LRU_WIDTH),
                               jax.nn.softplus(-p['lru_lambda'][l].astype(F32))], axis=0)

    log_g = jax.nn.log_sigmoid(p['ret_decay'][l].astype(F32))
    ret_lg = jnp.broadcast_to(_pad_rows(log_g, 8).T[:, :, None], (RET_HEADS, 8, RET_HEAD_DIM))

    return dict(
        ada_w=p['ada_w'][l].astype(BF16), ada_b=p['ada_b'][l][None],
        w_cat=w_cat.astype(BF16), w_dt=w_dt.astype(BF16),
        ret_lg=ret_lg, ret_gw=p['ret_gn_w'][l].reshape(RET_HEADS, 1, RET_HEAD_DIM),
        ret_gb=p['ret_gn_b'][l].reshape(RET_HEADS, 1, RET_HEAD_DIM),
        mla_qn=p['mla_q_norm'][l][None], mla_kn=p['mla_kv_norm'][l][None],
        mla_wq=wq.astype(BF16), mla_wkv=p['mla_w_ukv'][l].astype(BF16),
        lru_w=w_bd.astype(BF16), lru_vec=_pad_rows(lru_vec, 16),
        ssd_cv=_pad_rows(jnp.concatenate([p['ssd_conv_w'][l], p['ssd_conv_b'][l][None]], 0), 8),
        ssd_hp=ssd_hp, ssd_dsk=jnp.repeat(p['ssd_d'][l].astype(F32), SSD_HEAD_DIM)[None],
        ssd_nw=p['ssd_norm_w'][l][None],
        w_branch=p['w_branch'][l].astype(BF16), w_out=p['w_out'][l].astype(BF16),
        ln1=_pad_rows(jnp.stack([p['ln1_w'][l], p['ln1_b'][l]]), 8),
        ffn_wu=p['ffn_w_up'][l].astype(BF16), ffn_wd=p['ffn_w_down'][l].astype(BF16),
        ffn_cv=_pad_rows(jnp.concatenate([p['ffn_conv_w'][l], p['ffn_conv_b'][l][None]], 0), 8),
        ln2=_pad_rows(jnp.stack([p['ln2_w'][l], p['ln2_b'][l]]), 8),
    )


def kernel(x, c, ctx, c_ctx, ada_w, ada_b, w_in, ret_decay, ret_gn_w, ret_gn_b, mla_q_norm, mla_w_uq, mla_kv_norm, mla_w_ukv, lru_conv_w, lru_conv_b, lru_gate_w, lru_gate_b, lru_lambda, ssd_conv_w, ssd_conv_b, ssd_dt_bias, ssd_a_log, ssd_d, ssd_norm_w, w_branch, w_out, ln1_w, ln1_b, ffn_w_up, ffn_conv_w, ffn_conv_b, ffn_w_down, ln2_w, ln2_b):
    p = dict(ada_w=ada_w, ada_b=ada_b, w_in=w_in, ret_decay=ret_decay, ret_gn_w=ret_gn_w, ret_gn_b=ret_gn_b,
             mla_q_norm=mla_q_norm, mla_w_uq=mla_w_uq, mla_kv_norm=mla_kv_norm, mla_w_ukv=mla_w_ukv,
             lru_conv_w=lru_conv_w, lru_conv_b=lru_conv_b, lru_gate_w=lru_gate_w, lru_gate_b=lru_gate_b,
             lru_lambda=lru_lambda, ssd_conv_w=ssd_conv_w, ssd_conv_b=ssd_conv_b, ssd_dt_bias=ssd_dt_bias,
             ssd_a_log=ssd_a_log, ssd_d=ssd_d, ssd_norm_w=ssd_norm_w, w_branch=w_branch, w_out=w_out,
             ln1_w=ln1_w, ln1_b=ln1_b, ffn_w_up=ffn_w_up, ffn_conv_w=ffn_conv_w, ffn_conv_b=ffn_conv_b,
             ffn_w_down=ffn_w_down, ln2_w=ln2_w, ln2_b=ln2_b)
    batch, t_len, _ = x.shape
    n_ctx = ctx.shape[1]
    depth = ada_w.shape[0]
    assert n_ctx % ROW_TILE == 0 and t_len % ROW_TILE == 0 and t_len % GRID_W == 0
    alpha = (2 * depth) ** 0.25
    (ret_cos, ret_sin), (mla_cos, mla_sin) = _rope_tables(t_len, n_ctx)
    cond = _pad_rows(jnp.concatenate([c, c_ctx[None]], axis=0), -(-(batch + 1) // 8) * 8)
    h = jnp.concatenate([ctx, x], axis=1)
    for l in range(depth):
        with_ctx = l < depth - 1
        w = _layer_params(l, p)
        mods = _mod_call(cond, w['ada_w'], w['ada_b'])[:batch + 1].reshape(batch + 1, 6, D_MODEL)
        mods = jnp.pad(mods, ((0, 0), (0, 2), (0, 0)))
        p_gates, p_ret, p_ssd, p_lru, p_mla, p_dt = _inproj_call(h, mods, w['w_cat'], w['w_dt'], n_ctx)
        y_ret = _ret_call(p_ret, ret_cos, ret_sin, w['ret_lg'], w['ret_gw'], w['ret_gb'], n_ctx)
        q, k, v = _mla_prep_call(p_mla, w['mla_qn'], w['mla_kn'], w['mla_wq'], w['mla_wkv'], mla_cos, mla_sin)
        y_mla = _attn_call(q, k, v, n_ctx, with_ctx)
        y_lru = _lru_call(p_lru, w['lru_w'], w['lru_vec'], n_ctx)
        y_ssd = _ssd_call(p_ssd, p_dt, w['ssd_cv'], w['ssd_hp'], w['ssd_dsk'], n_ctx)
        h1 = _merge_call(p_gates, y_ret, y_mla, y_lru, y_ssd, h, mods, w['w_branch'], w['w_out'], w['ln1'], w['ssd_nw'],
                         n_ctx, with_ctx, alpha)
        h = _ffn_call(h1, mods, w['ffn_wu'], w['ffn_wd'], w['ffn_cv'], w['ln2'], n_ctx, with_ctx, alpha)
    return h
```

```python
import functools
import math

import jax
import jax.numpy as jnp
import numpy as np
from jax import lax
from jax.experimental import pallas as pl
from jax.experimental.pallas import tpu as pltpu

F32 = jnp.float32
BF16 = jnp.bfloat16

D_MODEL = 1024
GRID_W = 64
ROPE_BASE = 10000.0
LN_EPS = 1e-6
RMS_EPS = 1e-6

RET_HEADS = 4
RET_HEAD_DIM = 128
RET_WIDTH = RET_HEADS * RET_HEAD_DIM

MLA_HEADS = 4
MLA_Q_RANK = 384
MLA_KV_RANK = 256
MLA_NOPE = 128
MLA_ROPE = 64
MLA_V = 128
MLA_WIDTH = MLA_HEADS * MLA_V
MLA_QK_PAD = 256

LRU_WIDTH = 512
LRU_BLOCKS = 8
LRU_BLOCK = LRU_WIDTH // LRU_BLOCKS
LRU_C = 8.0

SSD_HEADS = 8
SSD_HEAD_DIM = 64
SSD_WIDTH = SSD_HEADS * SSD_HEAD_DIM
SSD_GROUPS = 2
SSD_STATE = 128
SSD_XBC = SSD_WIDTH + 2 * SSD_GROUPS * SSD_STATE
SSD_HG = SSD_HEADS // SSD_GROUPS

N_BRANCH = 4
D_FF = 2816

CHUNK = 128
LANES = 128
SUBLANES = 8
ROW_TILE = 256
ATTN_SUB = 256
N_CHUNK = 512
FF_CHUNK = 256
VMEM_LIMIT = 56 * 1024 * 1024

SEG_GATES = N_BRANCH * D_MODEL
SEG_RET = 4 * RET_WIDTH
SEG_SSD = SSD_XBC + SSD_WIDTH
SEG_LRU = 2 * LRU_WIDTH
SEG_MLA = 768
SEGS = (SEG_GATES, SEG_RET, SEG_SSD, SEG_LRU, SEG_MLA)
DT_COLS = SSD_GROUPS * LANES


def _dot(a, b):
    return jnp.dot(a, b, preferred_element_type=F32)


def _dot_nt(a, b):
    return lax.dot_general(a, b, (((1,), (1,)), ((), ())), preferred_element_type=F32)


def _split3(x):
    x1 = x.astype(BF16)
    r1 = x - x1.astype(F32)
    x2 = r1.astype(BF16)
    x3 = (r1 - x2.astype(F32)).astype(BF16)
    return x1, x2, x3


def _dot3_l(m, x):
    x1, x2, x3 = _split3(x)
    return _dot(m, x1) + _dot(m, x2) + _dot(m, x3)


def _dot3_r(x, m):
    x1, x2, x3 = _split3(x)
    return _dot(x1, m) + _dot(x2, m) + _dot(x3, m)


def _ln(x):
    mu = jnp.mean(x, axis=-1, keepdims=True)
    xc = x - mu
    var = jnp.mean(xc * xc, axis=-1, keepdims=True)
    return xc * lax.rsqrt(var + LN_EPS)


def _rms(x):
    return x * lax.rsqrt(jnp.mean(x * x, axis=-1, keepdims=True) + RMS_EPS)


def _sigmoid(x):
    return 0.5 * jnp.tanh(0.5 * x) + 0.5


def _silu(x):
    return x * _sigmoid(x)


def _softplus(x):
    return jnp.maximum(x, 0.0) + jnp.log1p(jnp.exp(-jnp.abs(x)))


def _gelu_tanh(x):
    return 0.5 * x * (1.0 + jnp.tanh(math.sqrt(2.0 / math.pi) * (x + 0.044715 * (x * x * x))))


def _iota(shape, dim):
    return lax.broadcasted_iota(jnp.int32, shape, dim)


def _largest_tile(n, candidates):
    return next(t for t in candidates if n % t == 0)


def _resident(shape):
    nd = len(shape)
    return pl.BlockSpec(shape, lambda *_: (0,) * nd, pipeline_mode=pl.Buffered(1))


def _params(*sem):
    return pltpu.CompilerParams(dimension_semantics=sem, vmem_limit_bytes=VMEM_LIMIT)


def _mod_kernel(c_ref, w_ref, b_ref, o_ref):
    s = _silu(c_ref[...])
    o_ref[...] = _dot(s.astype(BF16), w_ref[...]) + b_ref[...]


def _mod_call(cc, w, b):
    rows = cc.shape[0]
    n = w.shape[1]
    return pl.pallas_call(
        _mod_kernel,
        grid=(n // D_MODEL,),
        in_specs=[
            pl.BlockSpec((rows, D_MODEL), lambda j: (0, 0)),
            pl.BlockSpec((D_MODEL, D_MODEL), lambda j: (0, j)),
            pl.BlockSpec((1, D_MODEL), lambda j: (0, j)),
        ],
        out_specs=pl.BlockSpec((rows, D_MODEL), lambda j: (0, j)),
        out_shape=jax.ShapeDtypeStruct((rows, n), F32),
        compiler_params=_params("arbitrary"),
        name="adaln_mod",
    )(cc, w, b)


def _pick_mod(is_ctx, mx_ref, mc_ref, row):
    return jnp.where(is_ctx, mc_ref[0, row:row + 1, :], mx_ref[0, row:row + 1, :])


def _inproj_kernel(h_ref, mx_ref, mc_ref, w_ref, wdt_ref, og, orr, os_, ol, om, odt, *, ctx):
    tm = h_ref.shape[1]
    is_ctx = pl.program_id(1) * tm < ctx
    shift = _pick_mod(is_ctx, mx_ref, mc_ref, 0)
    scale = _pick_mod(is_ctx, mx_ref, mc_ref, 1)
    xm = (_ln(h_ref[0]) * (1.0 + scale) + shift).astype(BF16)
    col = 0
    for oref, width in zip((og, orr, os_, ol, om), SEGS):
        for c0 in range(0, width, N_CHUNK):
            cw = min(N_CHUNK, width - c0)
            oref[0, :, c0:c0 + cw] = _dot(xm, w_ref[:, col + c0:col + c0 + cw]).astype(oref.dtype)
        col += width
    odt[0] = _dot(xm, wdt_ref[...])


def _inproj_call(h, mods, w_cat, w_dt, ctx):
    b, s, _ = h.shape
    tm = ROW_TILE
    nb = mods.shape[0] - 1
    tok = lambda w: pl.BlockSpec((1, tm, w), lambda i, t: (i, t, 0))
    outs = [jax.ShapeDtypeStruct((b, s, w), BF16) for w in SEGS] + [jax.ShapeDtypeStruct((b, s, DT_COLS), F32)]
    return pl.pallas_call(
        functools.partial(_inproj_kernel, ctx=ctx),
        grid=(b, s // tm),
        in_specs=[
            tok(D_MODEL),
            pl.BlockSpec((1, 8, D_MODEL), lambda i, t: (i, 0, 0)),
            pl.BlockSpec((1, 8, D_MODEL), lambda i, t: (nb, 0, 0)),
            _resident(w_cat.shape),
            _resident(w_dt.shape),
        ],
        out_specs=[tok(w) for w in SEGS] + [tok(DT_COLS)],
        out_shape=outs,
        compiler_params=_params("parallel", "parallel"),
        name="in_proj",
    )(h, mods, mods, w_cat, w_dt)


def _ret_kernel(q_ref, k_ref, v_ref, g_ref, cos_ref, sin_ref, lg_ref, gw_ref, gb_ref, o_ref,
                y_scr, q_scr, kv_scr, st_scr, *, ctx):
    s_len = q_ref.shape[1]
    nck, nctx = s_len // CHUNK, ctx // CHUNK
    hd = RET_HEAD_DIM
    scale = hd ** -0.5
    lgf = lg_ref[0, 0:1, :]
    lgb = lg_ref[0, 1:2, :]
    d = (_iota((CHUNK, CHUNK), 0) - _iota((CHUNK, CHUNK), 1)).astype(F32)
    dec = (jnp.where(d >= 0, jnp.exp(jnp.maximum(d, 0.0) * lgf), 0.0)
           + jnp.where(d <= 0, jnp.exp(jnp.maximum(-d, 0.0) * lgb), 0.0))
    pos = _iota((CHUNK, 1), 0).astype(F32)
    posl = _iota((1, CHUNK), 1).astype(F32)
    qdec_f = jnp.exp((pos + 1.0) * lgf)
    qdec_b = jnp.exp((CHUNK - pos) * lgb)
    kdec_f = jnp.exp((CHUNK - 1.0 - posl) * lgf)
    kdec_b = jnp.exp(posl * lgb)
    cdec_f = jnp.exp(CHUNK * lgf)
    cdec_b = jnp.exp(CHUNK * lgb)
    gw = gw_ref[0]
    gb = gb_ref[0]

    def rows(n):
        return pl.ds(pl.multiple_of(n * CHUNK, CHUNK), CHUNK)

    def intra(n, carry):
        r = rows(n)
        cos = cos_ref[r, :]
        sin = sin_ref[r, :]
        q = q_ref[0, r, :].astype(F32)
        k = k_ref[0, r, :].astype(F32)
        v = v_ref[0, r, :]
        qb = (q * cos + pltpu.roll(q, hd // 2, 1) * sin).astype(BF16)
        kr = (k * cos + pltpu.roll(k, hd // 2, 1) * sin) * scale
        q_scr[r, :] = qb
        sc = _dot_nt(qb, kr.astype(BF16)) * dec
        y_scr[r, :] = _dot(sc.astype(BF16), v)
        kt = kr.T
        kv_scr[n] = _dot(jnp.concatenate([kt * kdec_f, kt * kdec_b], axis=0).astype(BF16), v)
        return carry

    lax.fori_loop(0, nck, intra, 0, unroll=3)

    def states(i, sts):
        sf, sb = sts
        st_scr[i, 0:hd, :] = sf.astype(BF16)
        sf = sf * cdec_f + kv_scr[i, 0:hd, :]
        j = jnp.where(i < nctx, nctx - 1 - i, nck - 1 - (i - nctx))
        st_scr[j, hd:2 * hd, :] = sb.astype(BF16)
        sb = sb * cdec_b + kv_scr[j, hd:2 * hd, :]
        return sf, sb

    zero = jnp.zeros((hd, hd), F32)
    lax.fori_loop(0, nck, states, (zero, zero))

    def inter(n, carry):
        r = rows(n)
        q = q_scr[r, :].astype(F32)
        qq = jnp.concatenate([q * qdec_f, q * qdec_b], axis=1).astype(BF16)
        y = y_scr[r, :] + _dot(qq, st_scr[n])
        g = g_ref[0, r, :].astype(F32)
        o_ref[0, r, :] = (_silu(g) * (_ln(y) * gw + gb)).astype(o_ref.dtype)
        return carry

    lax.fori_loop(0, nck, inter, 0, unroll=3)


def _ret_call(p_ret, cos, sin, lg, gw, gb, ctx):
    b, s, _ = p_ret.shape
    col = lambda off: pl.BlockSpec((1, s, RET_HEAD_DIM), lambda i, h: (i, 0, off + h))
    vec = pl.BlockSpec((1, 1, RET_HEAD_DIM), lambda i, h: (h, 0, 0))
    tab = pl.BlockSpec((s, RET_HEAD_DIM), lambda i, h: (0, 0))
    return pl.pallas_call(
        functools.partial(_ret_kernel, ctx=ctx),
        grid=(b, RET_HEADS),
        in_specs=[col(0), col(RET_HEADS), col(2 * RET_HEADS), col(3 * RET_HEADS), tab, tab,
                  pl.BlockSpec((1, 8, RET_HEAD_DIM), lambda i, h: (h, 0, 0)), vec, vec],
        out_specs=pl.BlockSpec((1, s, RET_HEAD_DIM), lambda i, h: (i, 0, h)),
        out_shape=jax.ShapeDtypeStruct((b, s, RET_WIDTH), BF16),
        scratch_shapes=[pltpu.VMEM((s, RET_HEAD_DIM), F32), pltpu.VMEM((s, RET_HEAD_DIM), BF16),
                        pltpu.VMEM((s // CHUNK, 2 * RET_HEAD_DIM, RET_HEAD_DIM), F32),
                        pltpu.VMEM((s // CHUNK, 2 * RET_HEAD_DIM, RET_HEAD_DIM), BF16)],
        compiler_params=_params("parallel", "parallel"),
        name="retention",
    )(p_ret, p_ret, p_ret, p_ret, cos, sin, lg, gw, gb)


def _mla_prep_kernel(m_ref, qn_ref, kn_ref, wq_ref, wkv_ref, cos_ref, sin_ref, q_ref, k_ref, v_ref):
    m = m_ref[0].astype(F32)
    cq = _rms(m[:, :MLA_Q_RANK]) * qn_ref[...]
    ckv = _rms(m[:, MLA_Q_RANK:MLA_Q_RANK + MLA_KV_RANK]) * kn_ref[...]
    kr = m[:, MLA_Q_RANK + MLA_KV_RANK:]
    q = _dot(cq.astype(BF16), wq_ref[...])
    kv = _dot(ckv.astype(BF16), wkv_ref[...])
    cos = cos_ref[...]
    sin = sin_ref[...]
    first_half = (_iota(cos.shape, 1) % MLA_ROPE) < (MLA_ROPE // 2)

    def rope(x):
        swapped = jnp.where(first_half, pltpu.roll(x, LANES - MLA_ROPE // 2, 1), pltpu.roll(x, MLA_ROPE // 2, 1))
        return x * cos + swapped * sin

    scale = (MLA_NOPE + MLA_ROPE) ** -0.5
    kr = rope(kr).astype(BF16)
    for h in range(MLA_HEADS):
        c0 = h * MLA_QK_PAD
        q_ref[0, h, :, :MLA_NOPE] = (q[:, c0:c0 + MLA_NOPE] * scale).astype(BF16)
        q_ref[0, h, :, MLA_NOPE:] = (rope(q[:, c0 + MLA_NOPE:c0 + MLA_QK_PAD]) * scale).astype(BF16)
        k_ref[0, h, :, :MLA_NOPE] = kv[:, c0:c0 + MLA_NOPE].astype(BF16)
        k_ref[0, h, :, MLA_NOPE:] = kr
        v_ref[0, h] = kv[:, c0 + MLA_NOPE:c0 + MLA_NOPE + MLA_V].T.astype(BF16)


def _mla_prep_call(p_mla, qn, kn, wq, wkv, cos, sin):
    b, s, _ = p_mla.shape
    tm = ROW_TILE
    head = lambda w: pl.BlockSpec((1, MLA_HEADS, tm, w), lambda i, t: (i, 0, t, 0))
    tab = pl.BlockSpec((tm, LANES), lambda i, t: (t, 0))
    return pl.pallas_call(
        _mla_prep_kernel,
        grid=(b, s // tm),
        in_specs=[pl.BlockSpec((1, tm, SEG_MLA), lambda i, t: (i, t, 0)),
                  _resident(qn.shape), _resident(kn.shape), _resident(wq.shape), _resident(wkv.shape), tab, tab],
        out_specs=[head(MLA_QK_PAD), head(MLA_QK_PAD),
                   pl.BlockSpec((1, MLA_HEADS, MLA_V, tm), lambda i, t: (i, 0, 0, t))],
        out_shape=[jax.ShapeDtypeStruct((b, MLA_HEADS, s, MLA_QK_PAD), BF16),
                   jax.ShapeDtypeStruct((b, MLA_HEADS, s, MLA_QK_PAD), BF16),
                   jax.ShapeDtypeStruct((b, MLA_HEADS, MLA_V, s), BF16)],
        compiler_params=_params("parallel", "parallel"),
        name="mla_prep",
    )(p_mla, qn, kn, wq, wkv, cos, sin)


def _attn_kernel(q_ref, k_ref, v_ref, o_ref, *, ctx):
    tq = q_ref.shape[2]
    s_len = k_ref.shape[2]
    row0 = pl.program_id(2) * tq

    def attend(sub, nk):
        rows = slice(sub * ATTN_SUB, (sub + 1) * ATTN_SUB)
        sc = _dot_nt(k_ref[0, 0, :nk, :], q_ref[0, 0, rows, :])
        p = jnp.exp(sc - jnp.max(sc, axis=0, keepdims=True))
        o_t = _dot(v_ref[0, 0, :, :nk], p.astype(BF16)) * (1.0 / jnp.sum(p, axis=0, keepdims=True))
        o_ref[0, rows, :] = o_t.T.astype(o_ref.dtype)

    def tile(n_ctx_sub):
        for sub in range(tq // ATTN_SUB):
            attend(sub, ctx if sub < n_ctx_sub else s_len)

    pl.when(row0 < ctx)(lambda: tile(ctx // ATTN_SUB))
    pl.when(row0 >= ctx)(lambda: tile(0))


def _attn_call(q, k, v, ctx):
    b, nh, s, _ = q.shape
    tq = _largest_tile(s, (768, 512, 256))
    assert ctx % ATTN_SUB == 0 and ctx <= tq
    return pl.pallas_call(
        functools.partial(_attn_kernel, ctx=ctx),
        grid=(b, nh, s // tq),
        in_specs=[pl.BlockSpec((1, 1, tq, MLA_QK_PAD), lambda i, h, t: (i, h, t, 0)),
                  pl.BlockSpec((1, 1, s, MLA_QK_PAD), lambda i, h, t: (i, h, 0, 0)),
                  pl.BlockSpec((1, 1, MLA_V, s), lambda i, h, t: (i, h, 0, 0))],
        out_specs=pl.BlockSpec((1, tq, MLA_V), lambda i, h, t: (i, t, h)),
        out_shape=jax.ShapeDtypeStruct((b, s, MLA_WIDTH), BF16),
        compiler_params=_params("parallel", "parallel", "arbitrary"),
        name="mla_attention",
    )(q, k, v)


def _padded_row(n, nctx):
    return pl.multiple_of(n * CHUNK + jnp.where(n < nctx, SUBLANES, 2 * SUBLANES), SUBLANES)


def _fill_padded(src_ref, col, pad_scr, *, ctx):
    s_len = src_ref.shape[1]
    width = pad_scr.shape[1]
    nctx = ctx // CHUNK
    zeros = jnp.zeros((SUBLANES, width), F32)
    pad_scr[0:SUBLANES, :] = zeros
    pad_scr[ctx + SUBLANES:ctx + 2 * SUBLANES, :] = zeros
    pad_scr[s_len + 2 * SUBLANES:s_len + 3 * SUBLANES, :] = zeros

    def body(n, carry):
        r = pl.ds(pl.multiple_of(n * CHUNK, CHUNK), CHUNK)
        pad_scr[pl.ds(_padded_row(n, nctx), CHUNK), :] = src_ref[0, r, col].astype(F32)
        return carry

    lax.fori_loop(0, s_len // CHUNK, body, 0)


def _conv4_all(pad_scr, taps_ref, dst_scr, act, *, ctx):
    s_len = dst_scr.shape[0]
    for n in range(s_len // CHUNK):
        base = n * CHUNK + (SUBLANES if n * CHUNK < ctx else 2 * SUBLANES)
        acc = taps_ref[4:5, :] + pad_scr[base:base + CHUNK, :] * taps_ref[1:2, :]
        acc = acc + pad_scr[base - 1:base - 1 + CHUNK, :] * taps_ref[0:1, :]
        acc = acc + pad_scr[base + 1:base + 1 + CHUNK, :] * taps_ref[2:3, :]
        acc = acc + pad_scr[base + 2:base + 2 + CHUNK, :] * taps_ref[3:4, :]
        dst_scr[n * CHUNK:(n + 1) * CHUNK, :] = act(acc)


def _lru_kernel(x_ref, g_ref, w_ref, vec_ref, o_ref, pad_scr, u_scr, af_scr, bf_scr, ab_scr, bb_scr,
                ta_scr, tb_scr, hin_scr, *, ctx):
    s_len = x_ref.shape[1]
    nck = s_len // CHUNK
    ntile, nctile = s_len // SUBLANES, ctx // SUBLANES
    ngrp, ncgrp = ntile // SUBLANES, nctile // SUBLANES
    _fill_padded(x_ref, slice(None), pad_scr, ctx=ctx)
    _conv4_all(pad_scr, vec_ref, u_scr, lambda v: v, ctx=ctx)

    def gates(n, carry):
        r = pl.ds(pl.multiple_of(n * CHUNK, CHUNK), CHUNK)
        u = u_scr[r, :]
        z = _dot(u.astype(BF16), w_ref[0])
        for d, (a_scr, b_scr) in enumerate(((af_scr, bf_scr), (ab_scr, bb_scr))):
            rg = _sigmoid(z[:, (2 * d) * LANES:(2 * d + 1) * LANES] + vec_ref[5 + 2 * d:6 + 2 * d, :])
            ig = _sigmoid(z[:, (2 * d + 1) * LANES:(2 * d + 2) * LANES] + vec_ref[6 + 2 * d:7 + 2 * d, :])
            log_a = -LRU_C * rg * vec_ref[9 + d:10 + d, :]
            a = jnp.exp(log_a)
            a_scr[r, :] = a
            b_scr[r, :] = jnp.sqrt(-jnp.tanh(log_a) * (1.0 + a * a)) * (ig * u)
        return carry

    lax.fori_loop(0, nck, gates, 0, unroll=2)

    part = ntile // 4
    assert part % SUBLANES == 0

    def tile_rows(k, p):
        return pl.ds(p * part * SUBLANES + k, part, stride=SUBLANES)

    for d, (a_scr, b_scr) in enumerate(((af_scr, bf_scr), (ab_scr, bb_scr))):
        ks = range(SUBLANES) if d == 0 else range(SUBLANES - 1, -1, -1)
        for p in range(4):
            pa = qa = None
            for k in ks:
                a_k = a_scr[tile_rows(k, p), :]
                b_k = b_scr[tile_rows(k, p), :]
                if pa is None:
                    pa, qa = a_k, b_k
                else:
                    pa, qa = a_k * pa, a_k * qa + b_k
                    a_scr[tile_rows(k, p), :] = pa
                    b_scr[tile_rows(k, p), :] = qa
            rows = slice(p * part, (p + 1) * part)
            ta_scr[d, rows, :] = pa
            tb_scr[d, rows, :] = qa

    rt = _iota((ntile, LANES), 0) % SUBLANES
    zero = jnp.zeros((SUBLANES, LANES), F32)
    hin_scr[0, 0:SUBLANES, :] = zero
    hin_scr[1, 0:SUBLANES, :] = zero
    for d in range(2):
        pa = ta_scr[d]
        qa = tb_scr[d]
        for sh in (1, 2, 4):
            if d == 0:
                keep = rt >= sh
                a_s, b_s = pltpu.roll(pa, sh, 0), pltpu.roll(qa, sh, 0)
            else:
                keep = rt < SUBLANES - sh
                a_s, b_s = pltpu.roll(pa, ntile - sh, 0), pltpu.roll(qa, ntile - sh, 0)
            qa = qa + pa * jnp.where(keep, b_s, 0.0)
            pa = pa * jnp.where(keep, a_s, 1.0)
        order = range(ngrp) if d == 0 else list(range(ncgrp - 1, -1, -1)) + list(range(ngrp - 1, ncgrp - 1, -1))
        h = zero
        for g in order:
            rows = slice(g * SUBLANES, (g + 1) * SUBLANES)
            t = pa[rows] * h + qa[rows]
            hin_scr[d, SUBLANES + g * SUBLANES:2 * SUBLANES + g * SUBLANES, :] = t
            last = t[SUBLANES - 1:SUBLANES] if d == 0 else t[0:1]
            h = jnp.broadcast_to(last, (SUBLANES, LANES))
    hin_scr[1, SUBLANES + ntile:SUBLANES + ntile + 1, :] = hin_scr[1, SUBLANES:SUBLANES + 1, :]

    tile_id = _iota((part, LANES), 0)
    for d, (a_scr, b_scr) in enumerate(((af_scr, bf_scr), (ab_scr, bb_scr))):
        for p in range(4):
            if d == 0:
                h_in = hin_scr[0, SUBLANES - 1 + p * part:SUBLANES - 1 + (p + 1) * part, :]
            else:
                h_in = hin_scr[1, SUBLANES + 1 + p * part:SUBLANES + 1 + (p + 1) * part, :]
                h_in = jnp.where(tile_id + p * part == nctile - 1, 0.0, h_in)
            for k in range(SUBLANES):
                b_scr[tile_rows(k, p), :] = a_scr[tile_rows(k, p), :] * h_in + b_scr[tile_rows(k, p), :]

    def finish(n, carry):
        r = pl.ds(pl.multiple_of(n * CHUNK, CHUNK), CHUNK)
        g = g_ref[0, r, :].astype(F32)
        o_ref[0, r, :] = ((bf_scr[r, :] + bb_scr[r, :]) * _gelu_tanh(g)).astype(o_ref.dtype)
        return carry

    lax.fori_loop(0, nck, finish, 0)


def _lru_call(p_lru, w_bd, vec, ctx):
    b, s, _ = p_lru.shape
    ncb = LRU_WIDTH // LANES
    return pl.pallas_call(
        functools.partial(_lru_kernel, ctx=ctx),
        grid=(b, ncb),
        in_specs=[pl.BlockSpec((1, s, LANES), lambda i, j: (i, 0, j)),
                  pl.BlockSpec((1, s, LANES), lambda i, j: (i, 0, ncb + j)),
                  pl.BlockSpec((1, LANES, 4 * LANES), lambda i, j: (j, 0, 0)),
                  pl.BlockSpec((16, LANES), lambda i, j: (0, j))],
        out_specs=pl.BlockSpec((1, s, LANES), lambda i, j: (i, 0, j)),
        out_shape=jax.ShapeDtypeStruct((b, s, LRU_WIDTH), BF16),
        scratch_shapes=[pltpu.VMEM((s + 3 * SUBLANES, LANES), F32)] + [pltpu.VMEM((s, LANES), F32)] * 5
        + [pltpu.VMEM((2, s // SUBLANES, LANES), F32)] * 2
        + [pltpu.VMEM((2, s // SUBLANES + 2 * SUBLANES, LANES), F32)],
        compiler_params=_params("parallel", "parallel"),
        name="rg_lru",
    )(p_lru, p_lru, w_bd, vec)


def _expand_heads(m, lane0, rows):
    width = SSD_HG * SSD_HEAD_DIM
    lane = _iota((rows, width), 1)
    out = jnp.broadcast_to(m[:, lane0 + SSD_HG - 1:lane0 + SSD_HG], (rows, width))
    for h in range(SSD_HG - 2, -1, -1):
        out = jnp.where(lane < (h + 1) * SSD_HEAD_DIM, m[:, lane0 + h:lane0 + h + 1], out)
    return out


def _ssd_kernel(x_ref, b_ref, c_ref, z_ref, dt_ref, cvx_ref, cvb_ref, cvc_ref, hp_ref, dsk_ref, o_ref,
                padx, padb, padc, xs_scr, bs_scr, cs_scr, y_scr, mf_scr, mb_scr, ds_scr, cd_scr, st_scr, sf_scr, sb_scr,
                *, ctx):
    s_len = x_ref.shape[1]
    nck, nctx = s_len // CHUNK, ctx // CHUNK
    gw = SSD_HG * SSD_HEAD_DIM
    _fill_padded(x_ref, slice(None), padx, ctx=ctx)
    _fill_padded(b_ref, slice(None), padb, ctx=ctx)
    _fill_padded(c_ref, slice(None), padc, ctx=ctx)

    _conv4_all(padx, cvx_ref, xs_scr, _silu, ctx=ctx)
    _conv4_all(padb, cvb_ref, bs_scr, _silu, ctx=ctx)
    _conv4_all(padc, cvc_ref, cs_scr, _silu, ctx=ctx)

    ii = _iota((CHUNK, CHUNK), 0)
    jj = _iota((CHUNK, CHUNK), 1)
    tri_l = (jj <= ii).astype(BF16)
    spread = (_iota((LANES, 2 * gw), 0) == _iota((LANES, 2 * gw), 1) // SSD_HEAD_DIM).astype(BF16)

    def expand(m):
        hi = m.astype(BF16)
        lo = (m - hi.astype(F32)).astype(BF16)
        return _dot(hi, spread) + _dot(lo, spread)

    lane = _iota((CHUNK, LANES), 1)
    lane256 = _iota((CHUNK, gw), 1)
    is_fwd_lane = lane < SSD_HG
    dt_bias = hp_ref[0, 0:1, :]
    a_neg = hp_ref[0, 1:2, :]

    def rows(n):
        return pl.ds(pl.multiple_of(n * CHUNK, CHUNK), CHUNK)

    def intra(n):
        r = rows(n)
        dt = _softplus(dt_ref[0, r, :] + dt_bias)
        dta = dt * a_neg
        af = _dot3_l(tri_l, dta)
        tot = af[CHUNK - 1:CHUNK, :]
        rv = tot - af + dta
        cum = jnp.where(is_fwd_lane, af, rv)
        cum_t = cum.T
        dt_t = dt.T
        xg = xs_scr[r, :]
        bsf = bs_scr[r, :]
        cg = cs_scr[r, :].astype(BF16)
        cb = _dot_nt(cg, bsf.astype(BF16))
        y = jnp.zeros((CHUNK, gw), F32)
        for h in range(SSD_HG):
            hb = SSD_HG + h
            arg = jnp.where(ii >= jj, cum[:, h:h + 1] - cum_t[h:h + 1, :], cum[:, hb:hb + 1] - cum_t[hb:hb + 1, :])
            wdt = jnp.where(ii > jj, dt_t[h:h + 1, :],
                            jnp.where(ii < jj, dt_t[hb:hb + 1, :], dt_t[h:h + 1, :] + dt_t[hb:hb + 1, :]))
            w = (cb * jnp.exp(arg) * wdt).astype(BF16)
            in_head = (lane256 >= h * SSD_HEAD_DIM) & (lane256 < (h + 1) * SSD_HEAD_DIM)
            y = y + _dot(w, jnp.where(in_head, xg, 0.0).astype(BF16))
        mult = expand(jnp.exp(cum))
        edge = jnp.where(is_fwd_lane[0:1], tot, rv[0:1, :])
        wgt = expand(jnp.exp(edge - cum) * dt)
        bt = bsf.T.astype(BF16)
        ds_f = _dot(bt, (xg * wgt[:, 0:gw]).astype(BF16))
        ds_b = _dot(bt, (xg * wgt[:, gw:2 * gw]).astype(BF16))
        return y, mult, ds_f, ds_b, jnp.broadcast_to(jnp.exp(edge), (SUBLANES, LANES))

    def intra_pair(i, carry):
        res = [(n, intra(n)) for n in (2 * i, 2 * i + 1)]
        for n, (y, mult, ds_f, ds_b, cd) in res:
            r = rows(n)
            y_scr[r, :] = y
            mf_scr[r, :] = mult[:, 0:gw]
            mb_scr[r, :] = mult[:, gw:2 * gw]
            ds_scr[n, :, 0:gw] = ds_f
            ds_scr[n, :, gw:2 * gw] = ds_b
            cd_scr[n] = cd
        return carry

    assert nck % 2 == 0
    lax.fori_loop(0, nck // 2, intra_pair, 0)

    sf_scr[...] = jnp.zeros(sf_scr.shape, F32)
    sb_scr[...] = jnp.zeros(sb_scr.shape, F32)

    def states(i, carry):
        sf = sf_scr[...]
        st_scr[i, :, 0:gw] = sf.astype(BF16)
        sf_scr[...] = sf * _expand_heads(cd_scr[i, 0:1, :], 0, 1) + ds_scr[i, :, 0:gw]
        j = jnp.where(i < nctx, nctx - 1 - i, nck - 1 - (i - nctx))
        sb = sb_scr[...]
        st_scr[j, :, gw:2 * gw] = sb.astype(BF16)
        sb_scr[...] = sb * _expand_heads(cd_scr[j, 0:1, :], SSD_HG, 1) + ds_scr[j, :, gw:2 * gw]
        return carry

    lax.fori_loop(0, nck, states, 0)

    def inter(n, carry):
        r = rows(n)
        ys = _dot(cs_scr[r, :].astype(BF16), st_scr[n])
        y = y_scr[r, :] + ys[:, 0:gw] * mf_scr[r, :] + ys[:, gw:2 * gw] * mb_scr[r, :] + dsk_ref[...] * xs_scr[r, :]
        o_ref[0, r, :] = (y * _silu(z_ref[0, r, :].astype(F32))).astype(o_ref.dtype)
        return carry

    lax.fori_loop(0, nck, inter, 0, unroll=2)


def _ssd_call(p_ssd, dt, cv, hp, dsk, ctx):
    b, s, _ = p_ssd.shape
    gw = SSD_HG * SSD_HEAD_DIM
    nx = SSD_WIDTH // LANES
    nbc = SSD_GROUPS * SSD_STATE // LANES
    pad = lambda w: pltpu.VMEM((s + 3 * SUBLANES, w), F32)
    return pl.pallas_call(
        functools.partial(_ssd_kernel, ctx=ctx),
        grid=(b, SSD_GROUPS),
        in_specs=[pl.BlockSpec((1, s, gw), lambda i, g: (i, 0, g)),
                  pl.BlockSpec((1, s, LANES), lambda i, g: (i, 0, nx + g)),
                  pl.BlockSpec((1, s, LANES), lambda i, g: (i, 0, nx + nbc + g)),
                  pl.BlockSpec((1, s, gw), lambda i, g: (i, 0, SSD_XBC // gw + g)),
                  pl.BlockSpec((1, s, LANES), lambda i, g: (i, 0, g)),
                  pl.BlockSpec((8, gw), lambda i, g: (0, g)),
                  pl.BlockSpec((8, LANES), lambda i, g: (0, nx + g)),
                  pl.BlockSpec((8, LANES), lambda i, g: (0, nx + nbc + g)),
                  pl.BlockSpec((1, 8, LANES), lambda i, g: (g, 0, 0)),
                  pl.BlockSpec((1, gw), lambda i, g: (0, g))],
        out_specs=pl.BlockSpec((1, s, gw), lambda i, g: (i, 0, g)),
        out_shape=jax.ShapeDtypeStruct((b, s, SSD_WIDTH), BF16),
        scratch_shapes=[pad(gw), pad(LANES), pad(LANES),
                        pltpu.VMEM((s, gw), F32), pltpu.VMEM((s, LANES), F32), pltpu.VMEM((s, LANES), F32),
                        pltpu.VMEM((s, gw), F32), pltpu.VMEM((s, gw), F32), pltpu.VMEM((s, gw), F32),
                        pltpu.VMEM((s // CHUNK, SSD_STATE, 2 * gw), F32), pltpu.VMEM((s // CHUNK, SUBLANES, LANES), F32),
                        pltpu.VMEM((s // CHUNK, SSD_STATE, 2 * gw), BF16),
                        pltpu.VMEM((SSD_STATE, gw), F32), pltpu.VMEM((SSD_STATE, gw), F32)],
        compiler_params=_params("parallel", "parallel"),
        name="ssd",
    )(p_ssd, p_ssd, p_ssd, p_ssd, dt, cv, cv, cv, hp, dsk)


def _merge_kernel(g_ref, ret_ref, mla_ref, lru_ref, ssd_ref, h_ref, mx_ref, mc_ref, wb_ref, wo_ref, vec_ref, nw_ref,
                  o_ref, *, ctx, first_tile, alpha):
    tm = h_ref.shape[1]
    is_ctx = (pl.program_id(1) + first_tile) * tm < ctx
    ssd = (_rms(ssd_ref[0].astype(F32)) * nw_ref[...]).astype(BF16)
    acc = None
    for i, br in enumerate((ret_ref[0], mla_ref[0], lru_ref[0], ssd)):
        gate = _sigmoid(g_ref[0, :, i * D_MODEL:(i + 1) * D_MODEL].astype(F32))
        term = gate * _dot(br, wb_ref[i])
        acc = term if acc is None else acc + term
    o = _dot(acc.astype(BF16), wo_ref[...])
    y = _ln(alpha * h_ref[0] + _pick_mod(is_ctx, mx_ref, mc_ref, 2) * o)
    o_ref[0] = y * vec_ref[0:1, :] + vec_ref[1:2, :]


def _merge_call(p_gates, y_ret, y_mla, y_lru, y_ssd, h, mods, wb, wo, vec, nw, ctx, with_ctx, alpha):
    b, s, _ = h.shape
    tm = ROW_TILE
    first = 0 if with_ctx else ctx // tm
    nb = mods.shape[0] - 1
    tok = lambda w: pl.BlockSpec((1, tm, w), lambda i, t: (i, t + first, 0))
    return pl.pallas_call(
        functools.partial(_merge_kernel, ctx=ctx, first_tile=first, alpha=alpha),
        grid=(b, s // tm - first),
        in_specs=[tok(SEG_GATES), tok(RET_WIDTH), tok(MLA_WIDTH), tok(LRU_WIDTH), tok(SSD_WIDTH), tok(D_MODEL),
                  pl.BlockSpec((1, 8, D_MODEL), lambda i, t: (i, 0, 0)),
                  pl.BlockSpec((1, 8, D_MODEL), lambda i, t: (nb, 0, 0)),
                  _resident(wb.shape), _resident(wo.shape), _resident(vec.shape), _resident(nw.shape)],
        out_specs=pl.BlockSpec((1, tm, D_MODEL), lambda i, t: (i, t, 0)),
        out_shape=jax.ShapeDtypeStruct((b, s - first * tm, D_MODEL), F32),
        compiler_params=_params("parallel", "parallel"),
        name="merge_postnorm",
    )(p_gates, y_ret, y_mla, y_lru, y_ssd, h, mods, mods, wb, wo, vec, nw)


def _ffn_kernel(h_ref, hp_ref, hn_ref, mx_ref, mc_ref, wu_ref, wd_ref, cv_ref, vec_ref, o_ref, *, ctx, s_len, alpha):
    tm = h_ref.shape[1]
    t = pl.program_id(1)
    row0 = t * tm
    h = h_ref[0]
    in_ctx = row0 + _iota((tm, 1), 0) < ctx

    def pick(row, is_ctx):
        if ctx == 0:
            return mx_ref[0, row:row + 1, :]
        return jnp.where(is_ctx, mc_ref[0, row:row + 1, :], mx_ref[0, row:row + 1, :])

    def mod(x, is_ctx):
        return _ln(x) * (1.0 + pick(4, is_ctx)) + pick(3, is_ctx)

    prev_ok = jnp.logical_and(row0 != 0, row0 != ctx)
    next_ok = jnp.logical_and(row0 + tm != ctx, row0 + tm != s_len)
    xm = jnp.concatenate([jnp.where(prev_ok, mod(hp_ref[0], row0 - 1 < ctx), 0.0), mod(h, in_ctx),
                          jnp.where(next_ok, mod(hn_ref[0], row0 + tm < ctx), 0.0)], axis=0).astype(BF16)
    rows = tm + 2 * SUBLANES
    lo, hi = SUBLANES, SUBLANES + tm
    seam = ctx % tm
    seam_tile = ctx // tm
    seam_row = _iota((2 * SUBLANES, 1), 0)

    def conv(u, c0):
        w_prev, w_mid, w_next = (cv_ref[k:k + 1, c0:c0 + FF_CHUNK] for k in range(3))
        out = cv_ref[3:4, c0:c0 + FF_CHUNK] + u[lo:hi] * w_mid
        out = out + pltpu.roll(u, 1, 0)[lo:hi] * w_prev
        out = out + pltpu.roll(u, rows - 1, 0)[lo:hi] * w_next
        if seam:
            slab = u[seam:seam + 2 * SUBLANES]
            fix = (jnp.where(seam_row == SUBLANES - 1, pltpu.roll(slab, 2 * SUBLANES - 1, 0) * w_next, 0.0)
                   + jnp.where(seam_row == SUBLANES, pltpu.roll(slab, 1, 0) * w_prev, 0.0))
            fix = jnp.where(t == seam_tile, fix, 0.0)
            out = jnp.concatenate([out[:seam - SUBLANES], out[seam - SUBLANES:seam + SUBLANES] - fix,
                                   out[seam + SUBLANES:]], axis=0)
        return out

    acc = jnp.zeros((tm, D_MODEL), F32)
    for c0 in range(0, D_FF, FF_CHUNK):
        g = conv(_dot(xm, wu_ref[:, c0:c0 + FF_CHUNK]), c0)
        v = conv(_dot(xm, wu_ref[:, D_FF + c0:D_FF + c0 + FF_CHUNK]), D_FF + c0)
        acc = acc + _dot((_silu(g) * v).astype(BF16), wd_ref[c0:c0 + FF_CHUNK, :])
    y = _ln(alpha * h + pick(5, in_ctx) * acc)
    o_ref[0] = y * vec_ref[0:1, :] + vec_ref[1:2, :]


def _ffn_call(h, mods, wu, wd, cv, vec, ctx, alpha):
    b, s, _ = h.shape
    tm = _largest_tile(s, (768, 512, 256))
    assert ctx % SUBLANES == 0 and (ctx % tm == 0 or SUBLANES <= ctx % tm <= tm - SUBLANES)
    nb = mods.shape[0] - 1
    per = tm // SUBLANES
    hi_blk = s // SUBLANES - 1
    return pl.pallas_call(
        functools.partial(_ffn_kernel, ctx=ctx, s_len=s, alpha=alpha),
        grid=(b, s // tm),
        in_specs=[pl.BlockSpec((1, tm, D_MODEL), lambda i, t: (i, t, 0)),
                  pl.BlockSpec((1, SUBLANES, D_MODEL), lambda i, t: (i, jnp.maximum(t * per - 1, 0), 0)),
                  pl.BlockSpec((1, SUBLANES, D_MODEL), lambda i, t: (i, jnp.minimum((t + 1) * per, hi_blk), 0)),
                  pl.BlockSpec((1, 8, D_MODEL), lambda i, t: (i, 0, 0)),
                  pl.BlockSpec((1, 8, D_MODEL), lambda i, t: (nb, 0, 0)),
                  _resident(wu.shape), _resident(wd.shape), _resident(cv.shape), _resident(vec.shape)],
        out_specs=pl.BlockSpec((1, tm, D_MODEL), lambda i, t: (i, t, 0)),
        out_shape=jax.ShapeDtypeStruct((b, s, D_MODEL), F32),
        compiler_params=_params("parallel", "parallel"),
        name="conv_ffn_postnorm",
    )(h, h, h, mods, mods, wu, wd, cv, vec)


def _pad_rows(a, rows):
    return jnp.pad(a, ((0, rows - a.shape[0]),) + ((0, 0),) * (a.ndim - 1))


def _rope_tables(t_len, ctx):
    rows = t_len // GRID_W
    r, col = jnp.meshgrid(jnp.arange(rows, dtype=F32), jnp.arange(GRID_W, dtype=F32), indexing='ij')

    def tables(dim):
        quarter = dim // 4
        inv = ROPE_BASE ** (-jnp.arange(quarter, dtype=F32) / quarter)
        ang = jnp.concatenate([r.reshape(-1, 1) * inv, col.reshape(-1, 1) * inv], axis=-1)
        cos, sin = jnp.cos(ang), jnp.sin(ang)
        return jnp.concatenate([cos, cos], -1), jnp.concatenate([-sin, sin], -1)

    def with_ctx(cos, sin, width):
        cos = jnp.pad(cos, ((0, 0), (0, width - cos.shape[1])), constant_values=1.0)
        sin = jnp.pad(sin, ((0, 0), (0, width - sin.shape[1])))
        return (jnp.concatenate([jnp.ones((ctx, width), F32), cos], 0),
                jnp.concatenate([jnp.zeros((ctx, width), F32), sin], 0))

    return with_ctx(*tables(RET_HEAD_DIM), RET_HEAD_DIM), with_ctx(*tables(MLA_ROPE), LANES)


def _layer_params(l, p):
    w_in = p['w_in'][l]
    offs = np.cumsum((0, SEG_GATES, RET_WIDTH, RET_WIDTH, RET_WIDTH, RET_WIDTH, MLA_Q_RANK, MLA_KV_RANK, MLA_ROPE,
                      LRU_WIDTH, LRU_WIDTH, SSD_WIDTH, SSD_XBC, 2 * SSD_HEADS))
    piece = lambda i: w_in[:, offs[i]:offs[i + 1]]
    zeros = lambda n: jnp.zeros((D_MODEL, n), w_in.dtype)
    w_cat = jnp.concatenate([piece(0), piece(1), piece(2), piece(3), piece(4), piece(11), piece(10), piece(8), piece(9),
                             piece(5), piece(6), piece(7), zeros(SEG_MLA - MLA_Q_RANK - MLA_KV_RANK - MLA_ROPE)], axis=1)
    dt_w = piece(12).reshape(D_MODEL, 2, SSD_GROUPS, SSD_HG).transpose(0, 2, 1, 3).reshape(D_MODEL, SSD_GROUPS, 2 * SSD_HG)
    w_dt = jnp.pad(dt_w, ((0, 0), (0, 0), (0, LANES - 2 * SSD_HG))).reshape(D_MODEL, DT_COLS)

    def per_group(v):
        v = v.reshape(2, SSD_GROUPS, SSD_HG).transpose(1, 0, 2).reshape(SSD_GROUPS, 2 * SSD_HG)
        return jnp.pad(v, ((0, 0), (0, LANES - 2 * SSD_HG)))

    ssd_hp = jnp.stack([per_group(p['ssd_dt_bias'][l]), per_group(-jnp.exp(p['ssd_a_log'][l].astype(F32)))], axis=1)
    ssd_hp = jnp.pad(ssd_hp, ((0, 0), (0, 6), (0, 0)))

    wq = p['mla_w_uq'][l].reshape(MLA_Q_RANK, MLA_HEADS, MLA_NOPE + MLA_ROPE)
    wq = jnp.pad(wq, ((0, 0), (0, 0), (0, MLA_QK_PAD - MLA_NOPE - MLA_ROPE))).reshape(MLA_Q_RANK, MLA_HEADS * MLA_QK_PAD)

    gw = p['lru_gate_w'][l].reshape(4, LRU_WIDTH // LANES, 2, LRU_BLOCK, LRU_BLOCK)
    w_bd = jnp.einsum('ajpcd,pq->japcqd', gw, jnp.eye(2, dtype=gw.dtype)).reshape(LRU_WIDTH // LANES, 4, LANES, LANES)
    w_bd = w_bd.transpose(0, 2, 1, 3).reshape(LRU_WIDTH // LANES, LANES, 4 * LANES)
    lru_vec = jnp.concatenate([p['lru_conv_w'][l], p['lru_conv_b'][l][None], p['lru_gate_b'][l].reshape(4, LRU_WIDTH),
                               jax.nn.softplus(-p['lru_lambda'][l].astype(F32))], axis=0)

    log_g = jax.nn.log_sigmoid(p['ret_decay'][l].astype(F32))
    ret_lg = jnp.broadcast_to(_pad_rows(log_g, 8).T[:, :, None], (RET_HEADS, 8, RET_HEAD_DIM))

    return dict(
        ada_w=p['ada_w'][l].astype(BF16), ada_b=p['ada_b'][l][None],
        w_cat=w_cat.astype(BF16), w_dt=w_dt.astype(BF16),
        ret_lg=ret_lg, ret_gw=p['ret_gn_w'][l].reshape(RET_HEADS, 1, RET_HEAD_DIM),
        ret_gb=p['ret_gn_b'][l].reshape(RET_HEADS, 1, RET_HEAD_DIM),
        mla_qn=p['mla_q_norm'][l][None], mla_kn=p['mla_kv_norm'][l][None],
        mla_wq=wq.astype(BF16), mla_wkv=p['mla_w_ukv'][l].astype(BF16),
        lru_w=w_bd.astype(BF16), lru_vec=_pad_rows(lru_vec, 16),
        ssd_cv=_pad_rows(jnp.concatenate([p['ssd_conv_w'][l], p['ssd_conv_b'][l][None]], 0), 8),
        ssd_hp=ssd_hp, ssd_dsk=jnp.repeat(p['ssd_d'][l].astype(F32), SSD_HEAD_DIM)[None],
        ssd_nw=p['ssd_norm_w'][l][None],
        w_branch=p['w_branch'][l].astype(BF16), w_out=p['w_out'][l].astype(BF16),
        ln1=_pad_rows(jnp.stack([p['ln1_w'][l], p['ln1_b'][l]]), 8),
        ffn_wu=p['ffn_w_up'][l].astype(BF16), ffn_wd=p['ffn_w_down'][l].astype(BF16),
        ffn_cv=_pad_rows(jnp.concatenate([p['ffn_conv_w'][l], p['ffn_conv_b'][l][None]], 0), 8),
        ln2=_pad_rows(jnp.stack([p['ln2_w'][l], p['ln2_b'][l]]), 8),
    )


def kernel(x, c, ctx, c_ctx, ada_w, ada_b, w_in, ret_decay, ret_gn_w, ret_gn_b, mla_q_norm, mla_w_uq, mla_kv_norm, mla_w_ukv, lru_conv_w, lru_conv_b, lru_gate_w, lru_gate_b, lru_lambda, ssd_conv_w, ssd_conv_b, ssd_dt_bias, ssd_a_log, ssd_d, ssd_norm_w, w_branch, w_out, ln1_w, ln1_b, ffn_w_up, ffn_conv_w, ffn_conv_b, ffn_w_down, ln2_w, ln2_b):
    p = dict(ada_w=ada_w, ada_b=ada_b, w_in=w_in, ret_decay=ret_decay, ret_gn_w=ret_gn_w, ret_gn_b=ret_gn_b,
             mla_q_norm=mla_q_norm, mla_w_uq=mla_w_uq, mla_kv_norm=mla_kv_norm, mla_w_ukv=mla_w_ukv,
             lru_conv_w=lru_conv_w, lru_conv_b=lru_conv_b, lru_gate_w=lru_gate_w, lru_gate_b=lru_gate_b,
             lru_lambda=lru_lambda, ssd_conv_w=ssd_conv_w, ssd_conv_b=ssd_conv_b, ssd_dt_bias=ssd_dt_bias,
             ssd_a_log=ssd_a_log, ssd_d=ssd_d, ssd_norm_w=ssd_norm_w, w_branch=w_branch, w_out=w_out,
             ln1_w=ln1_w, ln1_b=ln1_b, ffn_w_up=ffn_w_up, ffn_conv_w=ffn_conv_w, ffn_conv_b=ffn_conv_b,
             ffn_w_down=ffn_w_down, ln2_w=ln2_w, ln2_b=ln2_b)
    batch, t_len, _ = x.shape
    n_ctx = ctx.shape[1]
    depth = ada_w.shape[0]
    assert n_ctx % ROW_TILE == 0 and t_len % ROW_TILE == 0 and t_len % GRID_W == 0
    alpha = (2 * depth) ** 0.25
    (ret_cos, ret_sin), (mla_cos, mla_sin) = _rope_tables(t_len, n_ctx)
    cond = _pad_rows(jnp.concatenate([c, c_ctx[None]], axis=0), -(-(batch + 1) // 8) * 8)
    h = jnp.concatenate([ctx, x], axis=1)
    for l in range(depth):
        with_ctx = l < depth - 1
        w = _layer_params(l, p)
        mods = _mod_call(cond, w['ada_w'], w['ada_b'])[:batch + 1].reshape(batch + 1, 6, D_MODEL)
        mods = jnp.pad(mods, ((0, 0), (0, 2), (0, 0)))
        p_gates, p_ret, p_ssd, p_lru, p_mla, p_dt = _inproj_call(h, mods, w['w_cat'], w['w_dt'], n_ctx)
        y_ret = _ret_call(p_ret, ret_cos, ret_sin, w['ret_lg'], w['ret_gw'], w['ret_gb'], n_ctx)
        q, k, v = _mla_prep_call(p_mla, w['mla_qn'], w['mla_kn'], w['mla_wq'], w['mla_wkv'], mla_cos, mla_sin)
        y_mla = _attn_call(q, k, v, n_ctx)
        y_lru = _lru_call(p_lru, w['lru_w'], w['lru_vec'], n_ctx)
        y_ssd = _ssd_call(p_ssd, p_dt, w['ssd_cv'], w['ssd_hp'], w['ssd_dsk'], n_ctx)
        h1 = _merge_call(p_gates, y_ret, y_mla, y_lru, y_ssd, h, mods, w['w_branch'], w['w_out'], w['ln1'], w['ssd_nw'],
                         n_ctx, with_ctx, alpha)
        h = _ffn_call(h1, mods, w['ffn_wu'], w['ffn_wd'], w['ffn_cv'], w['ln2'], n_ctx if with_ctx else 0, alpha)
    return h
```

```python
import functools
import math

import jax
import jax.numpy as jnp
import numpy as np
from jax import lax
from jax.experimental import pallas as pl
from jax.experimental.pallas import tpu as pltpu

F32 = jnp.float32
BF16 = jnp.bfloat16

D_MODEL = 1024
GRID_W = 64
ROPE_BASE = 10000.0
LN_EPS = 1e-6
RMS_EPS = 1e-6

RET_HEADS = 4
RET_HEAD_DIM = 128
RET_WIDTH = RET_HEADS * RET_HEAD_DIM

MLA_HEADS = 4
MLA_Q_RANK = 384
MLA_KV_RANK = 256
MLA_NOPE = 128
MLA_ROPE = 64
MLA_V = 128
MLA_WIDTH = MLA_HEADS * MLA_V
MLA_QK_PAD = 256

LRU_WIDTH = 512
LRU_BLOCKS = 8
LRU_BLOCK = LRU_WIDTH // LRU_BLOCKS
LRU_C = 8.0

SSD_HEADS = 8
SSD_HEAD_DIM = 64
SSD_WIDTH = SSD_HEADS * SSD_HEAD_DIM
SSD_GROUPS = 2
SSD_STATE = 128
SSD_XBC = SSD_WIDTH + 2 * SSD_GROUPS * SSD_STATE
SSD_HG = SSD_HEADS // SSD_GROUPS

N_BRANCH = 4
D_FF = 2816

CHUNK = 128
LANES = 128
SUBLANES = 8
ROW_TILE = 256
ATTN_SUB = 256
N_CHUNK = 512
FF_CHUNK = 256
VMEM_LIMIT = 56 * 1024 * 1024

SEG_GATES = N_BRANCH * D_MODEL
SEG_RET = 4 * RET_WIDTH
SEG_SSD = SSD_XBC + SSD_WIDTH
SEG_LRU = 2 * LRU_WIDTH
SEG_MLA = 768
SEGS = (SEG_GATES, SEG_RET, SEG_SSD, SEG_LRU, SEG_MLA)
DT_COLS = SSD_GROUPS * LANES


def _dot(a, b):
    return jnp.dot(a, b, preferred_element_type=F32)


def _dot_nt(a, b):
    return lax.dot_general(a, b, (((1,), (1,)), ((), ())), preferred_element_type=F32)


def _split3(x):
    x1 = x.astype(BF16)
    r1 = x - x1.astype(F32)
    x2 = r1.astype(BF16)
    x3 = (r1 - x2.astype(F32)).astype(BF16)
    return x1, x2, x3


def _dot3_l(m, x):
    x1, x2, x3 = _split3(x)
    return _dot(m, x1) + _dot(m, x2) + _dot(m, x3)


def _dot3_r(x, m):
    x1, x2, x3 = _split3(x)
    return _dot(x1, m) + _dot(x2, m) + _dot(x3, m)


def _ln(x):
    mu = jnp.mean(x, axis=-1, keepdims=True)
    xc = x - mu
    var = jnp.mean(xc * xc, axis=-1, keepdims=True)
    return xc * lax.rsqrt(var + LN_EPS)


def _rms(x):
    return x * lax.rsqrt(jnp.mean(x * x, axis=-1, keepdims=True) + RMS_EPS)


def _sigmoid(x):
    return 0.5 * jnp.tanh(0.5 * x) + 0.5


def _silu(x):
    return x * _sigmoid(x)


def _softplus(x):
    return jnp.maximum(x, 0.0) + jnp.log1p(jnp.exp(-jnp.abs(x)))


def _gelu_tanh(x):
    return 0.5 * x * (1.0 + jnp.tanh(math.sqrt(2.0 / math.pi) * (x + 0.044715 * (x * x * x))))


def _iota(shape, dim):
    return lax.broadcasted_iota(jnp.int32, shape, dim)


def _largest_tile(n, candidates):
    return next(t for t in candidates if n % t == 0)


def _resident(shape):
    nd = len(shape)
    return pl.BlockSpec(shape, lambda *_: (0,) * nd, pipeline_mode=pl.Buffered(1))


def _params(*sem):
    return pltpu.CompilerParams(dimension_semantics=sem, vmem_limit_bytes=VMEM_LIMIT)


def _mod_kernel(c_ref, w_ref, b_ref, o_ref):
    s = _silu(c_ref[...])
    o_ref[...] = _dot(s.astype(BF16), w_ref[...]) + b_ref[...]


def _mod_call(cc, w, b):
    rows = cc.shape[0]
    n = w.shape[1]
    return pl.pallas_call(
        _mod_kernel,
        grid=(n // D_MODEL,),
        in_specs=[
            pl.BlockSpec((rows, D_MODEL), lambda j: (0, 0)),
            pl.BlockSpec((D_MODEL, D_MODEL), lambda j: (0, j)),
            pl.BlockSpec((1, D_MODEL), lambda j: (0, j)),
        ],
        out_specs=pl.BlockSpec((rows, D_MODEL), lambda j: (0, j)),
        out_shape=jax.ShapeDtypeStruct((rows, n), F32),
        compiler_params=_params("arbitrary"),
        name="adaln_mod",
    )(cc, w, b)


def _pick_mod(is_ctx, mx_ref, mc_ref, row):
    return jnp.where(is_ctx, mc_ref[0, row:row + 1, :], mx_ref[0, row:row + 1, :])


def _inproj_kernel(h_ref, mx_ref, mc_ref, w_ref, wdt_ref, og, orr, os_, ol, om, odt, *, ctx):
    tm = h_ref.shape[1]
    is_ctx = pl.program_id(1) * tm < ctx
    shift = _pick_mod(is_ctx, mx_ref, mc_ref, 0)
    scale = _pick_mod(is_ctx, mx_ref, mc_ref, 1)
    xm = (_ln(h_ref[0]) * (1.0 + scale) + shift).astype(BF16)
    col = 0
    for oref, width in zip((og, orr, os_, ol, om), SEGS):
        for c0 in range(0, width, N_CHUNK):
            cw = min(N_CHUNK, width - c0)
            oref[0, :, c0:c0 + cw] = _dot(xm, w_ref[:, col + c0:col + c0 + cw]).astype(oref.dtype)
        col += width
    odt[0] = _dot(xm, wdt_ref[...])


def _inproj_call(h, mods, w_cat, w_dt, ctx):
    b, s, _ = h.shape
    tm = ROW_TILE
    nb = mods.shape[0] - 1
    tok = lambda w: pl.BlockSpec((1, tm, w), lambda i, t: (i, t, 0))
    outs = [jax.ShapeDtypeStruct((b, s, w), BF16) for w in SEGS] + [jax.ShapeDtypeStruct((b, s, DT_COLS), F32)]
    return pl.pallas_call(
        functools.partial(_inproj_kernel, ctx=ctx),
        grid=(b, s // tm),
        in_specs=[
            tok(D_MODEL),
            pl.BlockSpec((1, 8, D_MODEL), lambda i, t: (i, 0, 0)),
            pl.BlockSpec((1, 8, D_MODEL), lambda i, t: (nb, 0, 0)),
            _resident(w_cat.shape),
            _resident(w_dt.shape),
        ],
        out_specs=[tok(w) for w in SEGS] + [tok(DT_COLS)],
        out_shape=outs,
        compiler_params=_params("parallel", "parallel"),
        name="in_proj",
    )(h, mods, mods, w_cat, w_dt)


def _ret_kernel(q_ref, k_ref, v_ref, g_ref, cos_ref, sin_ref, lg_ref, gw_ref, gb_ref, o_ref,
                y_scr, q_scr, kv_scr, st_scr, *, ctx):
    s_len = q_ref.shape[1]
    nck, nctx = s_len // CHUNK, ctx // CHUNK
    hd = RET_HEAD_DIM
    scale = hd ** -0.5
    lgf = lg_ref[0, 0:1, :]
    lgb = lg_ref[0, 1:2, :]
    d = (_iota((CHUNK, CHUNK), 0) - _iota((CHUNK, CHUNK), 1)).astype(F32)
    dec = (jnp.where(d >= 0, jnp.exp(jnp.maximum(d, 0.0) * lgf), 0.0)
           + jnp.where(d <= 0, jnp.exp(jnp.maximum(-d, 0.0) * lgb), 0.0))
    pos = _iota((CHUNK, 1), 0).astype(F32)
    posl = _iota((1, CHUNK), 1).astype(F32)
    qdec_f = jnp.exp((pos + 1.0) * lgf)
    qdec_b = jnp.exp((CHUNK - pos) * lgb)
    kdec_f = jnp.exp((CHUNK - 1.0 - posl) * lgf)
    kdec_b = jnp.exp(posl * lgb)
    cdec_f = jnp.exp(CHUNK * lgf)
    cdec_b = jnp.exp(CHUNK * lgb)
    gw = gw_ref[0]
    gb = gb_ref[0]

    def rows(n):
        return pl.ds(pl.multiple_of(n * CHUNK, CHUNK), CHUNK)

    def intra(n, carry):
        r = rows(n)
        cos = cos_ref[r, :]
        sin = sin_ref[r, :]
        q = q_ref[0, r, :].astype(F32)
        k = k_ref[0, r, :].astype(F32)
        v = v_ref[0, r, :]
        qb = (q * cos + pltpu.roll(q, hd // 2, 1) * sin).astype(BF16)
        kr = (k * cos + pltpu.roll(k, hd // 2, 1) * sin) * scale
        q_scr[r, :] = qb
        sc = _dot_nt(qb, kr.astype(BF16)) * dec
        y_scr[r, :] = _dot(sc.astype(BF16), v)
        kt = kr.T
        kv_scr[n] = _dot(jnp.concatenate([kt * kdec_f, kt * kdec_b], axis=0).astype(BF16), v)
        return carry

    lax.fori_loop(0, nck, intra, 0, unroll=3)

    def states(i, sts):
        sf, sb = sts
        st_scr[i, 0:hd, :] = sf.astype(BF16)
        sf = sf * cdec_f + kv_scr[i, 0:hd, :]
        j = jnp.where(i < nctx, nctx - 1 - i, nck - 1 - (i - nctx))
        st_scr[j, hd:2 * hd, :] = sb.astype(BF16)
        sb = sb * cdec_b + kv_scr[j, hd:2 * hd, :]
        return sf, sb

    zero = jnp.zeros((hd, hd), F32)
    lax.fori_loop(0, nck, states, (zero, zero))

    def inter(n, carry):
        r = rows(n)
        q = q_scr[r, :].astype(F32)
        qq = jnp.concatenate([q * qdec_f, q * qdec_b], axis=1).astype(BF16)
        y = y_scr[r, :] + _dot(qq, st_scr[n])
        g = g_ref[0, r, :].astype(F32)
        o_ref[0, r, :] = (_silu(g) * (_ln(y) * gw + gb)).astype(o_ref.dtype)
        return carry

    lax.fori_loop(0, nck, inter, 0, unroll=3)


def _ret_call(p_ret, cos, sin, lg, gw, gb, ctx):
    b, s, _ = p_ret.shape
    col = lambda off: pl.BlockSpec((1, s, RET_HEAD_DIM), lambda i, h: (i, 0, off + h))
    vec = pl.BlockSpec((1, 1, RET_HEAD_DIM), lambda i, h: (h, 0, 0))
    tab = pl.BlockSpec((s, RET_HEAD_DIM), lambda i, h: (0, 0))
    return pl.pallas_call(
        functools.partial(_ret_kernel, ctx=ctx),
        grid=(b, RET_HEADS),
        in_specs=[col(0), col(RET_HEADS), col(2 * RET_HEADS), col(3 * RET_HEADS), tab, tab,
                  pl.BlockSpec((1, 8, RET_HEAD_DIM), lambda i, h: (h, 0, 0)), vec, vec],
        out_specs=pl.BlockSpec((1, s, RET_HEAD_DIM), lambda i, h: (i, 0, h)),
        out_shape=jax.ShapeDtypeStruct((b, s, RET_WIDTH), BF16),
        scratch_shapes=[pltpu.VMEM((s, RET_HEAD_DIM), F32), pltpu.VMEM((s, RET_HEAD_DIM), BF16),
                        pltpu.VMEM((s // CHUNK, 2 * RET_HEAD_DIM, RET_HEAD_DIM), F32),
                        pltpu.VMEM((s // CHUNK, 2 * RET_HEAD_DIM, RET_HEAD_DIM), BF16)],
        compiler_params=_params("parallel", "parallel"),
        name="retention",
    )(p_ret, p_ret, p_ret, p_ret, cos, sin, lg, gw, gb)


def _mla_prep_kernel(m_ref, qn_ref, kn_ref, wq_ref, wk_ref, wvt_ref, cos_ref, sin_ref, q_ref, k_ref, v_ref):
    m = m_ref[0].astype(F32)
    cq = _rms(m[:, :MLA_Q_RANK]) * qn_ref[...]
    ckv = (_rms(m[:, MLA_Q_RANK:MLA_Q_RANK + MLA_KV_RANK]) * kn_ref[...]).astype(BF16)
    kr = m[:, MLA_Q_RANK + MLA_KV_RANK:]
    q = _dot(cq.astype(BF16), wq_ref[...])
    kn = _dot(ckv, wk_ref[...])
    vt = _dot_nt(wvt_ref[...], ckv)
    cos = cos_ref[...]
    sin = sin_ref[...]
    first_half = (_iota(cos.shape, 1) % MLA_ROPE) < (MLA_ROPE // 2)

    def rope(x):
        swapped = jnp.where(first_half, pltpu.roll(x, LANES - MLA_ROPE // 2, 1), pltpu.roll(x, MLA_ROPE // 2, 1))
        return x * cos + swapped * sin

    scale = (MLA_NOPE + MLA_ROPE) ** -0.5
    kr = rope(kr).astype(BF16)
    for h in range(MLA_HEADS):
        c0 = h * MLA_QK_PAD
        q_ref[0, h, :, :MLA_NOPE] = (q[:, c0:c0 + MLA_NOPE] * scale).astype(BF16)
        q_ref[0, h, :, MLA_NOPE:] = (rope(q[:, c0 + MLA_NOPE:c0 + MLA_QK_PAD]) * scale).astype(BF16)
        k_ref[0, h, :, :MLA_NOPE] = kn[:, h * MLA_NOPE:(h + 1) * MLA_NOPE].astype(BF16)
        k_ref[0, h, :, MLA_NOPE:] = kr
        v_ref[0, h] = vt[h * MLA_V:(h + 1) * MLA_V, :].astype(BF16)


def _mla_prep_call(p_mla, qn, kn, wq, wk, wvt, cos, sin):
    b, s, _ = p_mla.shape
    tm = ROW_TILE
    head = lambda w: pl.BlockSpec((1, MLA_HEADS, tm, w), lambda i, t: (i, 0, t, 0))
    tab = pl.BlockSpec((tm, LANES), lambda i, t: (t, 0))
    return pl.pallas_call(
        _mla_prep_kernel,
        grid=(b, s // tm),
        in_specs=[pl.BlockSpec((1, tm, SEG_MLA), lambda i, t: (i, t, 0)),
                  _resident(qn.shape), _resident(kn.shape), _resident(wq.shape), _resident(wk.shape),
                  _resident(wvt.shape), tab, tab],
        out_specs=[head(MLA_QK_PAD), head(MLA_QK_PAD),
                   pl.BlockSpec((1, MLA_HEADS, MLA_V, tm), lambda i, t: (i, 0, 0, t))],
        out_shape=[jax.ShapeDtypeStruct((b, MLA_HEADS, s, MLA_QK_PAD), BF16),
                   jax.ShapeDtypeStruct((b, MLA_HEADS, s, MLA_QK_PAD), BF16),
                   jax.ShapeDtypeStruct((b, MLA_HEADS, MLA_V, s), BF16)],
        compiler_params=_params("parallel", "parallel"),
        name="mla_prep",
    )(p_mla, qn, kn, wq, wk, wvt, cos, sin)


def _attn_kernel(q_ref, k_ref, v_ref, o_ref, *, ctx):
    tq = q_ref.shape[2]
    s_len = k_ref.shape[2]
    row0 = pl.program_id(2) * tq

    def attend(sub, nk):
        rows = slice(sub * ATTN_SUB, (sub + 1) * ATTN_SUB)
        sc = _dot_nt(k_ref[0, 0, :nk, :], q_ref[0, 0, rows, :])
        p = jnp.exp(sc - jnp.max(sc, axis=0, keepdims=True))
        o_t = _dot(v_ref[0, 0, :, :nk], p.astype(BF16)) * (1.0 / jnp.sum(p, axis=0, keepdims=True))
        o_ref[0, rows, :] = o_t.T.astype(o_ref.dtype)

    def tile(n_ctx_sub):
        for sub in range(tq // ATTN_SUB):
            attend(sub, ctx if sub < n_ctx_sub else s_len)

    pl.when(row0 < ctx)(lambda: tile(ctx // ATTN_SUB))
    pl.when(row0 >= ctx)(lambda: tile(0))


def _attn_call(q, k, v, ctx):
    b, nh, s, _ = q.shape
    tq = _largest_tile(s, (768, 512, 256))
    assert ctx % ATTN_SUB == 0 and ctx <= tq
    return pl.pallas_call(
        functools.partial(_attn_kernel, ctx=ctx),
        grid=(b, nh, s // tq),
        in_specs=[pl.BlockSpec((1, 1, tq, MLA_QK_PAD), lambda i, h, t: (i, h, t, 0)),
                  pl.BlockSpec((1, 1, s, MLA_QK_PAD), lambda i, h, t: (i, h, 0, 0)),
                  pl.BlockSpec((1, 1, MLA_V, s), lambda i, h, t: (i, h, 0, 0))],
        out_specs=pl.BlockSpec((1, tq, MLA_V), lambda i, h, t: (i, t, h)),
        out_shape=jax.ShapeDtypeStruct((b, s, MLA_WIDTH), BF16),
        compiler_params=_params("parallel", "parallel", "arbitrary"),
        name="mla_attention",
    )(q, k, v)


def _padded_row(n, nctx):
    return pl.multiple_of(n * CHUNK + jnp.where(n < nctx, SUBLANES, 2 * SUBLANES), SUBLANES)


def _fill_padded(src_ref, col, pad_scr, *, ctx):
    s_len = src_ref.shape[1]
    width = pad_scr.shape[1]
    nctx = ctx // CHUNK
    zeros = jnp.zeros((SUBLANES, width), F32)
    pad_scr[0:SUBLANES, :] = zeros
    pad_scr[ctx + SUBLANES:ctx + 2 * SUBLANES, :] = zeros
    pad_scr[s_len + 2 * SUBLANES:s_len + 3 * SUBLANES, :] = zeros

    def body(n, carry):
        r = pl.ds(pl.multiple_of(n * CHUNK, CHUNK), CHUNK)
        pad_scr[pl.ds(_padded_row(n, nctx), CHUNK), :] = src_ref[0, r, col].astype(F32)
        return carry

    lax.fori_loop(0, s_len // CHUNK, body, 0)


def _conv4_all(pad_scr, taps_ref, dst_scr, act, *, ctx):
    s_len = dst_scr.shape[0]
    for n in range(s_len // CHUNK):
        base = n * CHUNK + (SUBLANES if n * CHUNK < ctx else 2 * SUBLANES)
        acc = taps_ref[4:5, :] + pad_scr[base:base + CHUNK, :] * taps_ref[1:2, :]
        acc = acc + pad_scr[base - 1:base - 1 + CHUNK, :] * taps_ref[0:1, :]
        acc = acc + pad_scr[base + 1:base + 1 + CHUNK, :] * taps_ref[2:3, :]
        acc = acc + pad_scr[base + 2:base + 2 + CHUNK, :] * taps_ref[3:4, :]
        dst_scr[n * CHUNK:(n + 1) * CHUNK, :] = act(acc)


def _lru_kernel(x_ref, g_ref, w_ref, vec_ref, o_ref, pad_scr, u_scr, af_scr, bf_scr, ab_scr, bb_scr,
                ta_scr, tb_scr, hin_scr, *, ctx):
    s_len = x_ref.shape[1]
    nck = s_len // CHUNK
    ntile, nctile = s_len // SUBLANES, ctx // SUBLANES
    ngrp, ncgrp = ntile // SUBLANES, nctile // SUBLANES
    _fill_padded(x_ref, slice(None), pad_scr, ctx=ctx)
    _conv4_all(pad_scr, vec_ref, u_scr, lambda v: v, ctx=ctx)

    def gates(n, carry):
        r = pl.ds(pl.multiple_of(n * CHUNK, CHUNK), CHUNK)
        u = u_scr[r, :]
        z = _dot(u.astype(BF16), w_ref[0])
        for d, (a_scr, b_scr) in enumerate(((af_scr, bf_scr), (ab_scr, bb_scr))):
            rg = _sigmoid(z[:, (2 * d) * LANES:(2 * d + 1) * LANES] + vec_ref[5 + 2 * d:6 + 2 * d, :])
            ig = _sigmoid(z[:, (2 * d + 1) * LANES:(2 * d + 2) * LANES] + vec_ref[6 + 2 * d:7 + 2 * d, :])
            log_a = -LRU_C * rg * vec_ref[9 + d:10 + d, :]
            a = jnp.exp(log_a)
            a_scr[r, :] = a
            b_scr[r, :] = jnp.sqrt(-jnp.tanh(log_a) * (1.0 + a * a)) * (ig * u)
        return carry

    lax.fori_loop(0, nck, gates, 0, unroll=2)

    part = ntile // 4
    assert part % SUBLANES == 0

    def tile_rows(k, p):
        return pl.ds(p * part * SUBLANES + k, part, stride=SUBLANES)

    for d, (a_scr, b_scr) in enumerate(((af_scr, bf_scr), (ab_scr, bb_scr))):
        ks = range(SUBLANES) if d == 0 else range(SUBLANES - 1, -1, -1)
        for p in range(4):
            pa = qa = None
            for k in ks:
                a_k = a_scr[tile_rows(k, p), :]
                b_k = b_scr[tile_rows(k, p), :]
                if pa is None:
                    pa, qa = a_k, b_k
                else:
                    pa, qa = a_k * pa, a_k * qa + b_k
                    a_scr[tile_rows(k, p), :] = pa
                    b_scr[tile_rows(k, p), :] = qa
            rows = slice(p * part, (p + 1) * part)
            ta_scr[d, rows, :] = pa
            tb_scr[d, rows, :] = qa

    rt = _iota((ntile, LANES), 0) % SUBLANES
    zero = jnp.zeros((SUBLANES, LANES), F32)
    hin_scr[0, 0:SUBLANES, :] = zero
    hin_scr[1, 0:SUBLANES, :] = zero
    for d in range(2):
        pa = ta_scr[d]
        qa = tb_scr[d]
        for sh in (1, 2, 4):
            if d == 0:
                keep = rt >= sh
                a_s, b_s = pltpu.roll(pa, sh, 0), pltpu.roll(qa, sh, 0)
            else:
                keep = rt < SUBLANES - sh
                a_s, b_s = pltpu.roll(pa, ntile - sh, 0), pltpu.roll(qa, ntile - sh, 0)
            qa = qa + pa * jnp.where(keep, b_s, 0.0)
            pa = pa * jnp.where(keep, a_s, 1.0)
        order = range(ngrp) if d == 0 else list(range(ncgrp - 1, -1, -1)) + list(range(ngrp - 1, ncgrp - 1, -1))
        h = zero
        for g in order:
            rows = slice(g * SUBLANES, (g + 1) * SUBLANES)
            t = pa[rows] * h + qa[rows]
            hin_scr[d, SUBLANES + g * SUBLANES:2 * SUBLANES + g * SUBLANES, :] = t
            last = t[SUBLANES - 1:SUBLANES] if d == 0 else t[0:1]
            h = jnp.broadcast_to(last, (SUBLANES, LANES))
    hin_scr[1, SUBLANES + ntile:SUBLANES + ntile + 1, :] = hin_scr[1, SUBLANES:SUBLANES + 1, :]

    tile_id = _iota((part, LANES), 0)
    for d, (a_scr, b_scr) in enumerate(((af_scr, bf_scr), (ab_scr, bb_scr))):
        for p in range(4):
            if d == 0:
                h_in = hin_scr[0, SUBLANES - 1 + p * part:SUBLANES - 1 + (p + 1) * part, :]
            else:
                h_in = hin_scr[1, SUBLANES + 1 + p * part:SUBLANES + 1 + (p + 1) * part, :]
                h_in = jnp.where(tile_id + p * part == nctile - 1, 0.0, h_in)
            for k in range(SUBLANES):
                b_scr[tile_rows(k, p), :] = a_scr[tile_rows(k, p), :] * h_in + b_scr[tile_rows(k, p), :]

    def finish(n, carry):
        r = pl.ds(pl.multiple_of(n * CHUNK, CHUNK), CHUNK)
        g = g_ref[0, r, :].astype(F32)
        o_ref[0, r, :] = ((bf_scr[r, :] + bb_scr[r, :]) * _gelu_tanh(g)).astype(o_ref.dtype)
        return carry

    lax.fori_loop(0, nck, finish, 0)


def _lru_call(p_lru, w_bd, vec, ctx):
    b, s, _ = p_lru.shape
    ncb = LRU_WIDTH // LANES
    return pl.pallas_call(
        functools.partial(_lru_kernel, ctx=ctx),
        grid=(b, ncb),
        in_specs=[pl.BlockSpec((1, s, LANES), lambda i, j: (i, 0, j)),
                  pl.BlockSpec((1, s, LANES), lambda i, j: (i, 0, ncb + j)),
                  pl.BlockSpec((1, LANES, 4 * LANES), lambda i, j: (j, 0, 0)),
                  pl.BlockSpec((16, LANES), lambda i, j: (0, j))],
        out_specs=pl.BlockSpec((1, s, LANES), lambda i, j: (i, 0, j)),
        out_shape=jax.ShapeDtypeStruct((b, s, LRU_WIDTH), BF16),
        scratch_shapes=[pltpu.VMEM((s + 3 * SUBLANES, LANES), F32)] + [pltpu.VMEM((s, LANES), F32)] * 5
        + [pltpu.VMEM((2, s // SUBLANES, LANES), F32)] * 2
        + [pltpu.VMEM((2, s // SUBLANES + 2 * SUBLANES, LANES), F32)],
        compiler_params=_params("parallel", "parallel"),
        name="rg_lru",
    )(p_lru, p_lru, w_bd, vec)


def _expand_heads(m, lane0, rows):
    width = SSD_HG * SSD_HEAD_DIM
    lane = _iota((rows, width), 1)
    out = jnp.broadcast_to(m[:, lane0 + SSD_HG - 1:lane0 + SSD_HG], (rows, width))
    for h in range(SSD_HG - 2, -1, -1):
        out = jnp.where(lane < (h + 1) * SSD_HEAD_DIM, m[:, lane0 + h:lane0 + h + 1], out)
    return out


def _ssd_kernel(x_ref, b_ref, c_ref, z_ref, dt_ref, cvx_ref, cvb_ref, cvc_ref, hp_ref, dsk_ref, o_ref,
                padx, padb, padc, xs_scr, bs_scr, cs_scr, y_scr, mf_scr, mb_scr, ds_scr, cd_scr, st_scr, sf_scr, sb_scr,
                cum_scr, cb_scr, cumt_scr, dtt_scr, *, ctx):
    s_len = x_ref.shape[1]
    nck, nctx = s_len // CHUNK, ctx // CHUNK
    gw = SSD_HG * SSD_HEAD_DIM
    _fill_padded(x_ref, slice(None), padx, ctx=ctx)
    _fill_padded(b_ref, slice(None), padb, ctx=ctx)
    _fill_padded(c_ref, slice(None), padc, ctx=ctx)

    _conv4_all(padx, cvx_ref, xs_scr, _silu, ctx=ctx)
    _conv4_all(padb, cvb_ref, bs_scr, _silu, ctx=ctx)
    _conv4_all(padc, cvc_ref, cs_scr, _silu, ctx=ctx)

    ii = _iota((CHUNK, CHUNK), 0)
    jj = _iota((CHUNK, CHUNK), 1)
    spread = (_iota((LANES, 2 * gw), 0) == _iota((LANES, 2 * gw), 1) // SSD_HEAD_DIM).astype(BF16)

    def expand(m):
        hi = m.astype(BF16)
        lo = (m - hi.astype(F32)).astype(BF16)
        return _dot(hi, spread) + _dot(lo, spread)

    lane256 = _iota((CHUNK, gw), 1)
    dt_bias = hp_ref[0, 0:1, :]
    a_neg = hp_ref[0, 1:2, :]

    def rows(n):
        return pl.ds(pl.multiple_of(n * CHUNK, CHUNK), CHUNK)

    pair = 2 * CHUNK
    assert nck % 2 == 0
    i2 = _iota((pair, pair), 0)
    j2 = _iota((pair, pair), 1)
    tri2 = ((j2 <= i2) & (j2 // CHUNK == i2 // CHUNK)).astype(BF16)
    first_chunk = _iota((pair, LANES), 0) < CHUNK
    is_fwd_lane = _iota((pair, LANES), 1) < SSD_HG

    def decays(i, carry):
        r2 = pl.ds(pl.multiple_of(i * pair, pair), pair)
        dt = _softplus(dt_ref[0, r2, :] + dt_bias)
        dta = dt * a_neg
        af = _dot3_l(tri2, dta)
        tot = jnp.where(first_chunk, af[CHUNK - 1:CHUNK, :], af[pair - 1:pair, :])
        rv = tot - af + dta
        cum = jnp.where(is_fwd_lane, af, rv)
        cum_scr[r2, :] = cum
        cum_t = cum.T
        dt_t = dt.T
        mult = expand(jnp.exp(cum))
        mf_scr[r2, :] = mult[:, 0:gw]
        mb_scr[r2, :] = mult[:, gw:2 * gw]
        edge = jnp.where(is_fwd_lane, tot, jnp.where(first_chunk, rv[0:1, :], rv[CHUNK:CHUNK + 1, :]))
        wgt = _dot((jnp.exp(edge - cum) * dt).astype(BF16), spread)
        xg = xs_scr[r2, :]
        xw_f = (xg * wgt[:, 0:gw]).astype(BF16)
        xw_b = (xg * wgt[:, gw:2 * gw]).astype(BF16)
        chunk_decay = jnp.exp(edge)
        for c in range(2):
            n = 2 * i + c
            r = rows(n)
            rc = slice(c * CHUNK, (c + 1) * CHUNK)
            cumt_scr[n] = cum_t[0:SUBLANES, rc]
            dtt_scr[n] = dt_t[0:SUBLANES, rc]
            bsf = bs_scr[r, :]
            cb_scr[r, :] = _dot_nt(cs_scr[r, :].astype(BF16), bsf.astype(BF16))
            bt = bsf.T.astype(BF16)
            ds_scr[n, :, 0:gw] = _dot(bt, xw_f[rc])
            ds_scr[n, :, gw:2 * gw] = _dot(bt, xw_b[rc])
            cd_scr[n] = jnp.broadcast_to(chunk_decay[c * CHUNK:c * CHUNK + 1, :], (SUBLANES, LANES))
        return carry

    lax.fori_loop(0, nck // 2, decays, 0)

    def intra(n, carry):
        r = rows(n)
        cum = cum_scr[r, :]
        cum_t = cumt_scr[n]
        dt_t = dtt_scr[n]
        cb = cb_scr[r, :]
        xg = xs_scr[r, :]
        y = jnp.zeros((CHUNK, gw), F32)
        for h in range(SSD_HG):
            hb = SSD_HG + h
            arg = jnp.where(ii >= jj, cum[:, h:h + 1] - cum_t[h:h + 1, :], cum[:, hb:hb + 1] - cum_t[hb:hb + 1, :])
            wdt = jnp.where(ii > jj, dt_t[h:h + 1, :],
                            jnp.where(ii < jj, dt_t[hb:hb + 1, :], dt_t[h:h + 1, :] + dt_t[hb:hb + 1, :]))
            w = (cb * jnp.exp(arg) * wdt).astype(BF16)
            in_head = (lane256 >= h * SSD_HEAD_DIM) & (lane256 < (h + 1) * SSD_HEAD_DIM)
            y = y + _dot(w, jnp.where(in_head, xg, 0.0).astype(BF16))
        y_scr[r, :] = y
        return carry

    lax.fori_loop(0, nck, intra, 0, unroll=2)

    sf_scr[...] = jnp.zeros(sf_scr.shape, F32)
    sb_scr[...] = jnp.zeros(sb_scr.shape, F32)

    def states(i, carry):
        sf = sf_scr[...]
        st_scr[i, :, 0:gw] = sf.astype(BF16)
        sf_scr[...] = sf * _expand_heads(cd_scr[i, 0:1, :], 0, 1) + ds_scr[i, :, 0:gw]
        j = jnp.where(i < nctx, nctx - 1 - i, nck - 1 - (i - nctx))
        sb = sb_scr[...]
        st_scr[j, :, gw:2 * gw] = sb.astype(BF16)
        sb_scr[...] = sb * _expand_heads(cd_scr[j, 0:1, :], SSD_HG, 1) + ds_scr[j, :, gw:2 * gw]
        return carry

    lax.fori_loop(0, nck, states, 0)

    def inter(n, carry):
        r = rows(n)
        ys = _dot(cs_scr[r, :].astype(BF16), st_scr[n])
        y = y_scr[r, :] + ys[:, 0:gw] * mf_scr[r, :] + ys[:, gw:2 * gw] * mb_scr[r, :] + dsk_ref[...] * xs_scr[r, :]
        o_ref[0, r, :] = (y * _silu(z_ref[0, r, :].astype(F32))).astype(o_ref.dtype)
        return carry

    lax.fori_loop(0, nck, inter, 0, unroll=2)


def _ssd_call(p_ssd, dt, cv, hp, dsk, ctx):
    b, s, _ = p_ssd.shape
    gw = SSD_HG * SSD_HEAD_DIM
    nx = SSD_WIDTH // LANES
    nbc = SSD_GROUPS * SSD_STATE // LANES
    pad = lambda w: pltpu.VMEM((s + 3 * SUBLANES, w), F32)
    return pl.pallas_call(
        functools.partial(_ssd_kernel, ctx=ctx),
        grid=(b, SSD_GROUPS),
        in_specs=[pl.BlockSpec((1, s, gw), lambda i, g: (i, 0, g)),
                  pl.BlockSpec((1, s, LANES), lambda i, g: (i, 0, nx + g)),
                  pl.BlockSpec((1, s, LANES), lambda i, g: (i, 0, nx + nbc + g)),
                  pl.BlockSpec((1, s, gw), lambda i, g: (i, 0, SSD_XBC // gw + g)),
                  pl.BlockSpec((1, s, LANES), lambda i, g: (i, 0, g)),
                  pl.BlockSpec((8, gw), lambda i, g: (0, g)),
                  pl.BlockSpec((8, LANES), lambda i, g: (0, nx + g)),
                  pl.BlockSpec((8, LANES), lambda i, g: (0, nx + nbc + g)),
                  pl.BlockSpec((1, 8, LANES), lambda i, g: (g, 0, 0)),
                  pl.BlockSpec((1, gw), lambda i, g: (0, g))],
        out_specs=pl.BlockSpec((1, s, gw), lambda i, g: (i, 0, g)),
        out_shape=jax.ShapeDtypeStruct((b, s, SSD_WIDTH), BF16),
        scratch_shapes=[pad(gw), pad(LANES), pad(LANES),
                        pltpu.VMEM((s, gw), F32), pltpu.VMEM((s, LANES), F32), pltpu.VMEM((s, LANES), F32),
                        pltpu.VMEM((s, gw), F32), pltpu.VMEM((s, gw), F32), pltpu.VMEM((s, gw), F32),
                        pltpu.VMEM((s // CHUNK, SSD_STATE, 2 * gw), F32), pltpu.VMEM((s // CHUNK, SUBLANES, LANES), F32),
                        pltpu.VMEM((s // CHUNK, SSD_STATE, 2 * gw), BF16),
                        pltpu.VMEM((SSD_STATE, gw), F32), pltpu.VMEM((SSD_STATE, gw), F32),
                        pltpu.VMEM((s, LANES), F32), pltpu.VMEM((s, LANES), F32),
                        pltpu.VMEM((s // CHUNK, SUBLANES, LANES), F32), pltpu.VMEM((s // CHUNK, SUBLANES, LANES), F32)],
        compiler_params=_params("parallel", "parallel"),
        name="ssd",
    )(p_ssd, p_ssd, p_ssd, p_ssd, dt, cv, cv, cv, hp, dsk)


def _merge_kernel(g_ref, ret_ref, mla_ref, lru_ref, ssd_ref, h_ref, mx_ref, mc_ref, wb_ref, wo_ref, vec_ref, nw_ref,
                  o_ref, *, ctx, first_tile, alpha):
    tm = h_ref.shape[1]
    is_ctx = (pl.program_id(1) + first_tile) * tm + _iota((tm, 1), 0) < ctx
    ssd = (_rms(ssd_ref[0].astype(F32)) * nw_ref[...]).astype(BF16)
    acc = None
    for i, br in enumerate((ret_ref[0], mla_ref[0], lru_ref[0], ssd)):
        gate = _sigmoid(g_ref[0, :, i * D_MODEL:(i + 1) * D_MODEL].astype(F32))
        term = gate * _dot(br, wb_ref[i])
        acc = term if acc is None else acc + term
    o = _dot(acc.astype(BF16), wo_ref[...])
    y = _ln(alpha * h_ref[0] + _pick_mod(is_ctx, mx_ref, mc_ref, 2) * o)
    o_ref[0] = y * vec_ref[0:1, :] + vec_ref[1:2, :]


def _merge_call(p_gates, y_ret, y_mla, y_lru, y_ssd, h, mods, wb, wo, vec, nw, ctx, with_ctx, alpha):
    b, s, _ = h.shape
    tm = _largest_tile(s, (768, 512, 256)) if with_ctx else ROW_TILE
    first = 0 if with_ctx else ctx // tm
    nb = mods.shape[0] - 1
    tok = lambda w: pl.BlockSpec((1, tm, w), lambda i, t: (i, t + first, 0))
    return pl.pallas_call(
        functools.partial(_merge_kernel, ctx=ctx, first_tile=first, alpha=alpha),
        grid=(b, s // tm - first),
        in_specs=[tok(SEG_GATES), tok(RET_WIDTH), tok(MLA_WIDTH), tok(LRU_WIDTH), tok(SSD_WIDTH), tok(D_MODEL),
                  pl.BlockSpec((1, 8, D_MODEL), lambda i, t: (i, 0, 0)),
                  pl.BlockSpec((1, 8, D_MODEL), lambda i, t: (nb, 0, 0)),
                  _resident(wb.shape), _resident(wo.shape), _resident(vec.shape), _resident(nw.shape)],
        out_specs=pl.BlockSpec((1, tm, D_MODEL), lambda i, t: (i, t, 0)),
        out_shape=jax.ShapeDtypeStruct((b, s - first * tm, D_MODEL), F32),
        compiler_params=_params("parallel", "parallel"),
        name="merge_postnorm",
    )(p_gates, y_ret, y_mla, y_lru, y_ssd, h, mods, mods, wb, wo, vec, nw)


def _ffn_kernel(h_ref, hp_ref, hn_ref, mx_ref, mc_ref, wu_ref, wd_ref, cv_ref, vec_ref, o_ref, *, ctx, s_len, alpha):
    tm = h_ref.shape[1]
    t = pl.program_id(1)
    row0 = t * tm
    h = h_ref[0]
    in_ctx = row0 + _iota((tm, 1), 0) < ctx

    def pick(row, is_ctx):
        if ctx == 0:
            return mx_ref[0, row:row + 1, :]
        return jnp.where(is_ctx, mc_ref[0, row:row + 1, :], mx_ref[0, row:row + 1, :])

    def mod(x, is_ctx):
        return _ln(x) * (1.0 + pick(4, is_ctx)) + pick(3, is_ctx)

    prev_ok = jnp.logical_and(row0 != 0, row0 != ctx)
    next_ok = jnp.logical_and(row0 + tm != ctx, row0 + tm != s_len)
    xm = jnp.concatenate([jnp.where(prev_ok, mod(hp_ref[0], row0 - 1 < ctx), 0.0), mod(h, in_ctx),
                          jnp.where(next_ok, mod(hn_ref[0], row0 + tm < ctx), 0.0)], axis=0).astype(BF16)
    rows = tm + 2 * SUBLANES
    lo, hi = SUBLANES, SUBLANES + tm
    seam = ctx % tm
    seam_tile = ctx // tm
    seam_row = _iota((2 * SUBLANES, 1), 0)

    def conv(u, c0):
        w_prev, w_mid, w_next = (cv_ref[k:k + 1, c0:c0 + FF_CHUNK] for k in range(3))
        out = cv_ref[3:4, c0:c0 + FF_CHUNK] + u[lo:hi] * w_mid
        out = out + pltpu.roll(u, 1, 0)[lo:hi] * w_prev
        out = out + pltpu.roll(u, rows - 1, 0)[lo:hi] * w_next
        if seam:
            slab = u[seam:seam + 2 * SUBLANES]
            fix = (jnp.where(seam_row == SUBLANES - 1, pltpu.roll(slab, 2 * SUBLANES - 1, 0) * w_next, 0.0)
                   + jnp.where(seam_row == SUBLANES, pltpu.roll(slab, 1, 0) * w_prev, 0.0))
            fix = jnp.where(t == seam_tile, fix, 0.0)
            out = jnp.concatenate([out[:seam - SUBLANES], out[seam - SUBLANES:seam + SUBLANES] - fix,
                                   out[seam + SUBLANES:]], axis=0)
        return out

    acc = jnp.zeros((tm, D_MODEL), F32)
    for c0 in range(0, D_FF, FF_CHUNK):
        g = conv(_dot(xm, wu_ref[:, c0:c0 + FF_CHUNK]), c0)
        v = conv(_dot(xm, wu_ref[:, D_FF + c0:D_FF + c0 + FF_CHUNK]), D_FF + c0)
        acc = acc + _dot((_silu(g) * v).astype(BF16), wd_ref[c0:c0 + FF_CHUNK, :])
    y = _ln(alpha * h + pick(5, in_ctx) * acc)
    o_ref[0] = y * vec_ref[0:1, :] + vec_ref[1:2, :]


def _ffn_call(h, mods, wu, wd, cv, vec, ctx, alpha):
    b, s, _ = h.shape
    tm = _largest_tile(s, (768, 512, 256))
    assert ctx % SUBLANES == 0 and (ctx % tm == 0 or SUBLANES <= ctx % tm <= tm - SUBLANES)
    nb = mods.shape[0] - 1
    per = tm // SUBLANES
    hi_blk = s // SUBLANES - 1
    return pl.pallas_call(
        functools.partial(_ffn_kernel, ctx=ctx, s_len=s, alpha=alpha),
        grid=(b, s // tm),
        in_specs=[pl.BlockSpec((1, tm, D_MODEL), lambda i, t: (i, t, 0)),
                  pl.BlockSpec((1, SUBLANES, D_MODEL), lambda i, t: (i, jnp.maximum(t * per - 1, 0), 0)),
                  pl.BlockSpec((1, SUBLANES, D_MODEL), lambda i, t: (i, jnp.minimum((t + 1) * per, hi_blk), 0)),
                  pl.BlockSpec((1, 8, D_MODEL), lambda i, t: (i, 0, 0)),
                  pl.BlockSpec((1, 8, D_MODEL), lambda i, t: (nb, 0, 0)),
                  _resident(wu.shape), _resident(wd.shape), _resident(cv.shape), _resident(vec.shape)],
        out_specs=pl.BlockSpec((1, tm, D_MODEL), lambda i, t: (i, t, 0)),
        out_shape=jax.ShapeDtypeStruct((b, s, D_MODEL), F32),
        compiler_params=_params("parallel", "parallel"),
        name="conv_ffn_postnorm",
    )(h, h, h, mods, mods, wu, wd, cv, vec)


def _pad_rows(a, rows):
    return jnp.pad(a, ((0, rows - a.shape[0]),) + ((0, 0),) * (a.ndim - 1))


def _rope_tables(t_len, ctx):
    rows = t_len // GRID_W
    r, col = jnp.meshgrid(jnp.arange(rows, dtype=F32), jnp.arange(GRID_W, dtype=F32), indexing='ij')

    def tables(dim):
        quarter = dim // 4
        inv = ROPE_BASE ** (-jnp.arange(quarter, dtype=F32) / quarter)
        ang = jnp.concatenate([r.reshape(-1, 1) * inv, col.reshape(-1, 1) * inv], axis=-1)
        cos, sin = jnp.cos(ang), jnp.sin(ang)
        return jnp.concatenate([cos, cos], -1), jnp.concatenate([-sin, sin], -1)

    def with_ctx(cos, sin, width):
        cos = jnp.pad(cos, ((0, 0), (0, width - cos.shape[1])), constant_values=1.0)
        sin = jnp.pad(sin, ((0, 0), (0, width - sin.shape[1])))
        return (jnp.concatenate([jnp.ones((ctx, width), F32), cos], 0),
                jnp.concatenate([jnp.zeros((ctx, width), F32), sin], 0))

    return with_ctx(*tables(RET_HEAD_DIM), RET_HEAD_DIM), with_ctx(*tables(MLA_ROPE), LANES)


def _layer_params(l, p):
    w_in = p['w_in'][l]
    offs = np.cumsum((0, SEG_GATES, RET_WIDTH, RET_WIDTH, RET_WIDTH, RET_WIDTH, MLA_Q_RANK, MLA_KV_RANK, MLA_ROPE,
                      LRU_WIDTH, LRU_WIDTH, SSD_WIDTH, SSD_XBC, 2 * SSD_HEADS))
    piece = lambda i: w_in[:, offs[i]:offs[i + 1]]
    zeros = lambda n: jnp.zeros((D_MODEL, n), w_in.dtype)
    w_cat = jnp.concatenate([piece(0), piece(1), piece(2), piece(3), piece(4), piece(11), piece(10), piece(8), piece(9),
                             piece(5), piece(6), piece(7), zeros(SEG_MLA - MLA_Q_RANK - MLA_KV_RANK - MLA_ROPE)], axis=1)
    dt_w = piece(12).reshape(D_MODEL, 2, SSD_GROUPS, SSD_HG).transpose(0, 2, 1, 3).reshape(D_MODEL, SSD_GROUPS, 2 * SSD_HG)
    w_dt = jnp.pad(dt_w, ((0, 0), (0, 0), (0, LANES - 2 * SSD_HG))).reshape(D_MODEL, DT_COLS)

    def per_group(v):
        v = v.reshape(2, SSD_GROUPS, SSD_HG).transpose(1, 0, 2).reshape(SSD_GROUPS, 2 * SSD_HG)
        return jnp.pad(v, ((0, 0), (0, LANES - 2 * SSD_HG)))

    ssd_hp = jnp.stack([per_group(p['ssd_dt_bias'][l]), per_group(-jnp.exp(p['ssd_a_log'][l].astype(F32)))], axis=1)
    ssd_hp = jnp.pad(ssd_hp, ((0, 0), (0, 6), (0, 0)))

    wq = p['mla_w_uq'][l].reshape(MLA_Q_RANK, MLA_HEADS, MLA_NOPE + MLA_ROPE)
    wq = jnp.pad(wq, ((0, 0), (0, 0), (0, MLA_QK_PAD - MLA_NOPE - MLA_ROPE))).reshape(MLA_Q_RANK, MLA_HEADS * MLA_QK_PAD)

    wkv = p['mla_w_ukv'][l].reshape(MLA_KV_RANK, MLA_HEADS, MLA_NOPE + MLA_V)

    gw = p['lru_gate_w'][l].reshape(4, LRU_WIDTH // LANES, 2, LRU_BLOCK, LRU_BLOCK)
    w_bd = jnp.einsum('ajpcd,pq->japcqd', gw, jnp.eye(2, dtype=gw.dtype)).reshape(LRU_WIDTH // LANES, 4, LANES, LANES)
    w_bd = w_bd.transpose(0, 2, 1, 3).reshape(LRU_WIDTH // LANES, LANES, 4 * LANES)
    lru_vec = jnp.concatenate([p['lru_conv_w'][l], p['lru_conv_b'][l][None], p['lru_gate_b'][l].reshape(4, LRU_WIDTH),
                               jax.nn.softplus(-p['lru_lambda'][l].astype(F32))], axis=0)

    log_g = jax.nn.log_sigmoid(p['ret_decay'][l].astype(F32))
    ret_lg = jnp.broadcast_to(_pad_rows(log_g, 8).T[:, :, None], (RET_HEADS, 8, RET_HEAD_DIM))

    return dict(
        ada_w=p['ada_w'][l].astype(BF16), ada_b=p['ada_b'][l][None],
        w_cat=w_cat.astype(BF16), w_dt=w_dt.astype(BF16),
        ret_lg=ret_lg, ret_gw=p['ret_gn_w'][l].reshape(RET_HEADS, 1, RET_HEAD_DIM),
        ret_gb=p['ret_gn_b'][l].reshape(RET_HEADS, 1, RET_HEAD_DIM),
        mla_qn=p['mla_q_norm'][l][None], mla_kn=p['mla_kv_norm'][l][None],
        mla_wq=wq.astype(BF16), mla_wk=wkv[:, :, :MLA_NOPE].reshape(MLA_KV_RANK, MLA_HEADS * MLA_NOPE).astype(BF16),
        mla_wvt=wkv[:, :, MLA_NOPE:].reshape(MLA_KV_RANK, MLA_HEADS * MLA_V).T.astype(BF16),
        lru_w=w_bd.astype(BF16), lru_vec=_pad_rows(lru_vec, 16),
        ssd_cv=_pad_rows(jnp.concatenate([p['ssd_conv_w'][l], p['ssd_conv_b'][l][None]], 0), 8),
        ssd_hp=ssd_hp, ssd_dsk=jnp.repeat(p['ssd_d'][l].astype(F32), SSD_HEAD_DIM)[None],
        ssd_nw=p['ssd_norm_w'][l][None],
        w_branch=p['w_branch'][l].astype(BF16), w_out=p['w_out'][l].astype(BF16),
        ln1=_pad_rows(jnp.stack([p['ln1_w'][l], p['ln1_b'][l]]), 8),
        ffn_wu=p['ffn_w_up'][l].astype(BF16), ffn_wd=p['ffn_w_down'][l].astype(BF16),
        ffn_cv=_pad_rows(jnp.concatenate([p['ffn_conv_w'][l], p['ffn_conv_b'][l][None]], 0), 8),
        ln2=_pad_rows(jnp.stack([p['ln2_w'][l], p['ln2_b'][l]]), 8),
    )


def kernel(x, c, ctx, c_ctx, ada_w, ada_b, w_in, ret_decay, ret_gn_w, ret_gn_b, mla_q_norm, mla_w_uq, mla_kv_norm, mla_w_ukv, lru_conv_w, lru_conv_b, lru_gate_w, lru_gate_b, lru_lambda, ssd_conv_w, ssd_conv_b, ssd_dt_bias, ssd_a_log, ssd_d, ssd_norm_w, w_branch, w_out, ln1_w, ln1_b, ffn_w_up, ffn_conv_w, ffn_conv_b, ffn_w_down, ln2_w, ln2_b):
    p = dict(ada_w=ada_w, ada_b=ada_b, w_in=w_in, ret_decay=ret_decay, ret_gn_w=ret_gn_w, ret_gn_b=ret_gn_b,
             mla_q_norm=mla_q_norm, mla_w_uq=mla_w_uq, mla_kv_norm=mla_kv_norm, mla_w_ukv=mla_w_ukv,
             lru_conv_w=lru_conv_w, lru_conv_b=lru_conv_b, lru_gate_w=lru_gate_w, lru_gate_b=lru_gate_b,
             lru_lambda=lru_lambda, ssd_conv_w=ssd_conv_w, ssd_conv_b=ssd_conv_b, ssd_dt_bias=ssd_dt_bias,
             ssd_a_log=ssd_a_log, ssd_d=ssd_d, ssd_norm_w=ssd_norm_w, w_branch=w_branch, w_out=w_out,
             ln1_w=ln1_w, ln1_b=ln1_b, ffn_w_up=ffn_w_up, ffn_conv_w=ffn_conv_w, ffn_conv_b=ffn_conv_b,
             ffn_w_down=ffn_w_down, ln2_w=ln2_w, ln2_b=ln2_b)
    batch, t_len, _ = x.shape
    n_ctx = ctx.shape[1]
    depth = ada_w.shape[0]
    assert n_ctx % ROW_TILE == 0 and t_len % ROW_TILE == 0 and t_len % GRID_W == 0
    alpha = (2 * depth) ** 0.25
    (ret_cos, ret_sin), (mla_cos, mla_sin) = _rope_tables(t_len, n_ctx)
    cond = _pad_rows(jnp.concatenate([c, c_ctx[None]], axis=0), -(-(batch + 1) // 8) * 8)
    h = jnp.concatenate([ctx, x], axis=1)
    for l in range(depth):
        with_ctx = l < depth - 1
        w = _layer_params(l, p)
        mods = _mod_call(cond, w['ada_w'], w['ada_b'])[:batch + 1].reshape(batch + 1, 6, D_MODEL)
        mods = jnp.pad(mods, ((0, 0), (0, 2), (0, 0)))
        p_gates, p_ret, p_ssd, p_lru, p_mla, p_dt = _inproj_call(h, mods, w['w_cat'], w['w_dt'], n_ctx)
        y_ret = _ret_call(p_ret, ret_cos, ret_sin, w['ret_lg'], w['ret_gw'], w['ret_gb'], n_ctx)
        q, k, v = _mla_prep_call(p_mla, w['mla_qn'], w['mla_kn'], w['mla_wq'], w['mla_wk'], w['mla_wvt'], mla_cos, mla_sin)
        y_mla = _attn_call(q, k, v, n_ctx)
        y_lru = _lru_call(p_lru, w['lru_w'], w['lru_vec'], n_ctx)
        y_ssd = _ssd_call(p_ssd, p_dt, w['ssd_cv'], w['ssd_hp'], w['ssd_dsk'], n_ctx)
        h1 = _merge_call(p_gates, y_ret, y_mla, y_lru, y_ssd, h, mods, w['w_branch'], w['w_out'], w['ln1'], w['ssd_nw'],
                         n_ctx, with_ctx, alpha)
        h = _ffn_call(h1, mods, w['ffn_wu'], w['ffn_wd'], w['ffn_cv'], w['ln2'], n_ctx if with_ctx else 0, alpha)
    return h
```

```python
import functools
import math

import jax
import jax.numpy as jnp
import numpy as np
from jax import lax
from jax.experimental import pallas as pl
from jax.experimental.pallas import tpu as pltpu

F32 = jnp.float32
BF16 = jnp.bfloat16

D_MODEL = 1024
GRID_W = 64
ROPE_BASE = 10000.0
LN_EPS = 1e-6
RMS_EPS = 1e-6

RET_HEADS = 4
RET_HEAD_DIM = 128
RET_WIDTH = RET_HEADS * RET_HEAD_DIM

MLA_HEADS = 4
MLA_Q_RANK = 384
MLA_KV_RANK = 256
MLA_NOPE = 128
MLA_ROPE = 64
MLA_V = 128
MLA_WIDTH = MLA_HEADS * MLA_V
MLA_QK_PAD = 256

LRU_WIDTH = 512
LRU_BLOCKS = 8
LRU_BLOCK = LRU_WIDTH // LRU_BLOCKS
LRU_C = 8.0

SSD_HEADS = 8
SSD_HEAD_DIM = 64
SSD_WIDTH = SSD_HEADS * SSD_HEAD_DIM
SSD_GROUPS = 2
SSD_STATE = 128
SSD_XBC = SSD_WIDTH + 2 * SSD_GROUPS * SSD_STATE
SSD_HG = SSD_HEADS // SSD_GROUPS

N_BRANCH = 4
D_FF = 2816

CHUNK = 128
LANES = 128
SUBLANES = 8
ROW_TILE = 256
ATTN_SUB = 256
N_CHUNK = 512
FF_CHUNK = 256
VMEM_LIMIT = 56 * 1024 * 1024

SEG_GATES = N_BRANCH * D_MODEL
SEG_RET = 4 * RET_WIDTH
SEG_SSD = SSD_XBC + SSD_WIDTH
SEG_LRU = 2 * LRU_WIDTH
SEG_MLA = 768
SEGS = (SEG_GATES, SEG_RET, SEG_SSD, SEG_LRU, SEG_MLA)
DT_COLS = SSD_GROUPS * LANES


def _dot(a, b):
    return jnp.dot(a, b, preferred_element_type=F32)


def _dot_nt(a, b):
    return lax.dot_general(a, b, (((1,), (1,)), ((), ())), preferred_element_type=F32)


def _split2(x):
    hi = x.astype(BF16)
    return hi, (x - hi.astype(F32)).astype(BF16)


def _dot2_l(m, x):
    hi, lo = _split2(x)
    return _dot(m, hi) + _dot(m, lo)


def _dot2_r(x, m):
    hi, lo = _split2(x)
    return _dot(hi, m) + _dot(lo, m)


def _ln(x):
    mu = jnp.mean(x, axis=-1, keepdims=True)
    xc = x - mu
    var = jnp.mean(xc * xc, axis=-1, keepdims=True)
    return xc * lax.rsqrt(var + LN_EPS)


def _rms(x):
    return x * lax.rsqrt(jnp.mean(x * x, axis=-1, keepdims=True) + RMS_EPS)


def _sigmoid(x):
    return 0.5 * jnp.tanh(0.5 * x) + 0.5


def _silu(x):
    return x * _sigmoid(x)


def _softplus(x):
    return jnp.maximum(x, 0.0) + jnp.log1p(jnp.exp(-jnp.abs(x)))


def _gelu_tanh(x):
    return 0.5 * x * (1.0 + jnp.tanh(math.sqrt(2.0 / math.pi) * (x + 0.044715 * (x * x * x))))


def _iota(shape, dim):
    return lax.broadcasted_iota(jnp.int32, shape, dim)


def _largest_tile(n, candidates):
    return next(t for t in candidates if n % t == 0)


def _resident(shape):
    nd = len(shape)
    return pl.BlockSpec(shape, lambda *_: (0,) * nd, pipeline_mode=pl.Buffered(1))


def _params(*sem):
    return pltpu.CompilerParams(dimension_semantics=sem, vmem_limit_bytes=VMEM_LIMIT)


def _mod_kernel(c_ref, w_ref, b_ref, o_ref):
    s = _silu(c_ref[...])
    o_ref[...] = _dot(s.astype(BF16), w_ref[...]) + b_ref[...]


def _mod_call(cc, w, b):
    rows = cc.shape[0]
    n = w.shape[1]
    return pl.pallas_call(
        _mod_kernel,
        grid=(n // D_MODEL,),
        in_specs=[
            pl.BlockSpec((rows, D_MODEL), lambda j: (0, 0)),
            pl.BlockSpec((D_MODEL, D_MODEL), lambda j: (0, j)),
            pl.BlockSpec((1, D_MODEL), lambda j: (0, j)),
        ],
        out_specs=pl.BlockSpec((rows, D_MODEL), lambda j: (0, j)),
        out_shape=jax.ShapeDtypeStruct((rows, n), F32),
        compiler_params=_params("arbitrary"),
        name="adaln_mod",
    )(cc, w, b)


def _pick_mod(is_ctx, mx_ref, mc_ref, row):
    return jnp.where(is_ctx, mc_ref[0, row:row + 1, :], mx_ref[0, row:row + 1, :])


def _inproj_kernel(h_ref, mx_ref, mc_ref, w_ref, wdt_ref, og, orr, os_, ol, om, odt, *, ctx):
    tm = h_ref.shape[1]
    is_ctx = pl.program_id(1) * tm < ctx
    shift = _pick_mod(is_ctx, mx_ref, mc_ref, 0)
    scale = _pick_mod(is_ctx, mx_ref, mc_ref, 1)
    xm = (_ln(h_ref[0]) * (1.0 + scale) + shift).astype(BF16)
    col = 0
    for oref, width in zip((og, orr, os_, ol, om), SEGS):
        for c0 in range(0, width, N_CHUNK):
            cw = min(N_CHUNK, width - c0)
            oref[0, :, c0:c0 + cw] = _dot(xm, w_ref[:, col + c0:col + c0 + cw]).astype(oref.dtype)
        col += width
    odt[0] = _dot(xm, wdt_ref[...])


def _inproj_call(h, mods, w_cat, w_dt, ctx):
    b, s, _ = h.shape
    tm = ROW_TILE
    nb = mods.shape[0] - 1
    tok = lambda w: pl.BlockSpec((1, tm, w), lambda i, t: (i, t, 0))
    outs = [jax.ShapeDtypeStruct((b, s, w), BF16) for w in SEGS] + [jax.ShapeDtypeStruct((b, s, DT_COLS), F32)]
    return pl.pallas_call(
        functools.partial(_inproj_kernel, ctx=ctx),
        grid=(b, s // tm),
        in_specs=[
            tok(D_MODEL),
            pl.BlockSpec((1, 8, D_MODEL), lambda i, t: (i, 0, 0)),
            pl.BlockSpec((1, 8, D_MODEL), lambda i, t: (nb, 0, 0)),
            _resident(w_cat.shape),
            _resident(w_dt.shape),
        ],
        out_specs=[tok(w) for w in SEGS] + [tok(DT_COLS)],
        out_shape=outs,
        compiler_params=_params("parallel", "parallel"),
        name="in_proj",
    )(h, mods, mods, w_cat, w_dt)


def _ret_kernel(q_ref, k_ref, v_ref, g_ref, cos_ref, sin_ref, lg_ref, gw_ref, gb_ref, o_ref,
                y_scr, q_scr, kv_scr, st_scr, *, ctx):
    s_len = q_ref.shape[1]
    nck, nctx = s_len // CHUNK, ctx // CHUNK
    hd = RET_HEAD_DIM
    scale = hd ** -0.5
    lgf = lg_ref[0, 0:1, :]
    lgb = lg_ref[0, 1:2, :]
    d = (_iota((CHUNK, CHUNK), 0) - _iota((CHUNK, CHUNK), 1)).astype(F32)
    dec = (jnp.where(d >= 0, jnp.exp(jnp.maximum(d, 0.0) * lgf), 0.0)
           + jnp.where(d <= 0, jnp.exp(jnp.maximum(-d, 0.0) * lgb), 0.0))
    pos = _iota((CHUNK, 1), 0).astype(F32)
    posl = _iota((1, CHUNK), 1).astype(F32)
    qdec_f = jnp.exp((pos + 1.0) * lgf)
    qdec_b = jnp.exp((CHUNK - pos) * lgb)
    kdec_f = jnp.exp((CHUNK - 1.0 - posl) * lgf)
    kdec_b = jnp.exp(posl * lgb)
    cdec_f = jnp.exp(CHUNK * lgf)
    cdec_b = jnp.exp(CHUNK * lgb)
    gw = gw_ref[0]
    gb = gb_ref[0]

    def rows(n):
        if isinstance(n, int):
            return pl.ds(n * CHUNK, CHUNK)
        return pl.ds(pl.multiple_of(n * CHUNK, CHUNK), CHUNK)

    def intra(n, carry):
        r = rows(n)
        cos = cos_ref[r, :]
        sin = sin_ref[r, :]
        q = q_ref[0, r, :].astype(F32)
        k = k_ref[0, r, :].astype(F32)
        v = v_ref[0, r, :]
        qb = (q * cos + pltpu.roll(q, hd // 2, 1) * sin).astype(BF16)
        kr = (k * cos + pltpu.roll(k, hd // 2, 1) * sin) * scale
        q_scr[r, :] = qb
        sc = _dot_nt(qb, kr.astype(BF16)) * dec
        y_scr[r, :] = _dot(sc.astype(BF16), v)
        kt = kr.T
        kv_scr[n] = _dot(jnp.concatenate([kt * kdec_f, kt * kdec_b], axis=0).astype(BF16), v)
        return carry

    for n in range(nck):
        intra(n, 0)

    def states(i, sts):
        sf, sb = sts
        st_scr[i, 0:hd, :] = sf.astype(BF16)
        sf = sf * cdec_f + kv_scr[i, 0:hd, :]
        j = jnp.where(i < nctx, nctx - 1 - i, nck - 1 - (i - nctx))
        st_scr[j, hd:2 * hd, :] = sb.astype(BF16)
        sb = sb * cdec_b + kv_scr[j, hd:2 * hd, :]
        return sf, sb

    zero = jnp.zeros((hd, hd), F32)
    lax.fori_loop(0, nck, states, (zero, zero))

    def inter(n, carry):
        r = rows(n)
        q = q_scr[r, :].astype(F32)
        qq = jnp.concatenate([q * qdec_f, q * qdec_b], axis=1).astype(BF16)
        y = y_scr[r, :] + _dot(qq, st_scr[n])
        g = g_ref[0, r, :].astype(F32)
        o_ref[0, r, :] = (_silu(g) * (_ln(y) * gw + gb)).astype(o_ref.dtype)
        return carry

    lax.fori_loop(0, nck, inter, 0, unroll=6)


def _ret_call(p_ret, cos, sin, lg, gw, gb, ctx):
    b, s, _ = p_ret.shape
    col = lambda off: pl.BlockSpec((1, s, RET_HEAD_DIM), lambda i, h: (i, 0, off + h))
    vec = pl.BlockSpec((1, 1, RET_HEAD_DIM), lambda i, h: (h, 0, 0))
    tab = pl.BlockSpec((s, RET_HEAD_DIM), lambda i, h: (0, 0))
    return pl.pallas_call(
        functools.partial(_ret_kernel, ctx=ctx),
        grid=(b, RET_HEADS),
        in_specs=[col(0), col(RET_HEADS), col(2 * RET_HEADS), col(3 * RET_HEADS), tab, tab,
                  pl.BlockSpec((1, 8, RET_HEAD_DIM), lambda i, h: (h, 0, 0)), vec, vec],
        out_specs=pl.BlockSpec((1, s, RET_HEAD_DIM), lambda i, h: (i, 0, h)),
        out_shape=jax.ShapeDtypeStruct((b, s, RET_WIDTH), BF16),
        scratch_shapes=[pltpu.VMEM((s, RET_HEAD_DIM), F32), pltpu.VMEM((s, RET_HEAD_DIM), BF16),
                        pltpu.VMEM((s // CHUNK, 2 * RET_HEAD_DIM, RET_HEAD_DIM), F32),
                        pltpu.VMEM((s // CHUNK, 2 * RET_HEAD_DIM, RET_HEAD_DIM), BF16)],
        compiler_params=_params("parallel", "parallel"),
        name="retention",
    )(p_ret, p_ret, p_ret, p_ret, cos, sin, lg, gw, gb)


def _mla_prep_kernel(m_ref, qn_ref, kn_ref, wq_ref, wk_ref, wvt_ref, cos_ref, sin_ref, q_ref, k_ref, v_ref):
    m = m_ref[0].astype(F32)
    cq = _rms(m[:, :MLA_Q_RANK]) * qn_ref[...]
    ckv = (_rms(m[:, MLA_Q_RANK:MLA_Q_RANK + MLA_KV_RANK]) * kn_ref[...]).astype(BF16)
    kr = m[:, MLA_Q_RANK + MLA_KV_RANK:]
    q = _dot(cq.astype(BF16), wq_ref[...])
    kn = _dot(ckv, wk_ref[...])
    vt = _dot_nt(wvt_ref[...], ckv)
    cos = cos_ref[...]
    sin = sin_ref[...]
    first_half = (_iota(cos.shape, 1) % MLA_ROPE) < (MLA_ROPE // 2)

    def rope(x):
        swapped = jnp.where(first_half, pltpu.roll(x, LANES - MLA_ROPE // 2, 1), pltpu.roll(x, MLA_ROPE // 2, 1))
        return x * cos + swapped * sin

    scale = (MLA_NOPE + MLA_ROPE) ** -0.5
    kr = rope(kr).astype(BF16)
    for h in range(MLA_HEADS):
        c0 = h * MLA_QK_PAD
        q_ref[0, h, :, :MLA_NOPE] = (q[:, c0:c0 + MLA_NOPE] * scale).astype(BF16)
        q_ref[0, h, :, MLA_NOPE:] = (rope(q[:, c0 + MLA_NOPE:c0 + MLA_QK_PAD]) * scale).astype(BF16)
        k_ref[0, h, :, :MLA_NOPE] = kn[:, h * MLA_NOPE:(h + 1) * MLA_NOPE].astype(BF16)
        k_ref[0, h, :, MLA_NOPE:] = kr
        v_ref[0, h] = vt[h * MLA_V:(h + 1) * MLA_V, :].astype(BF16)


def _mla_prep_call(p_mla, qn, kn, wq, wk, wvt, cos, sin):
    b, s, _ = p_mla.shape
    tm = _largest_tile(s, (768, 512, 256))
    head = lambda w: pl.BlockSpec((1, MLA_HEADS, tm, w), lambda i, t: (i, 0, t, 0))
    tab = pl.BlockSpec((tm, LANES), lambda i, t: (t, 0))
    return pl.pallas_call(
        _mla_prep_kernel,
        grid=(b, s // tm),
        in_specs=[pl.BlockSpec((1, tm, SEG_MLA), lambda i, t: (i, t, 0)),
                  _resident(qn.shape), _resident(kn.shape), _resident(wq.shape), _resident(wk.shape),
                  _resident(wvt.shape), tab, tab],
        out_specs=[head(MLA_QK_PAD), head(MLA_QK_PAD),
                   pl.BlockSpec((1, MLA_HEADS, MLA_V, tm), lambda i, t: (i, 0, 0, t))],
        out_shape=[jax.ShapeDtypeStruct((b, MLA_HEADS, s, MLA_QK_PAD), BF16),
                   jax.ShapeDtypeStruct((b, MLA_HEADS, s, MLA_QK_PAD), BF16),
                   jax.ShapeDtypeStruct((b, MLA_HEADS, MLA_V, s), BF16)],
        compiler_params=_params("parallel", "parallel"),
        name="mla_prep",
    )(p_mla, qn, kn, wq, wk, wvt, cos, sin)


def _attn_kernel(q_ref, k_ref, v_ref, o_ref, *, ctx):
    tq = q_ref.shape[2]
    s_len = k_ref.shape[2]
    row0 = pl.program_id(2) * tq

    def attend(sub, nk):
        rows = slice(sub * ATTN_SUB, (sub + 1) * ATTN_SUB)
        sc = _dot_nt(k_ref[0, 0, :nk, :], q_ref[0, 0, rows, :])
        p = jnp.exp(sc - jnp.max(sc, axis=0, keepdims=True))
        o_t = _dot(v_ref[0, 0, :, :nk], p.astype(BF16)) * (1.0 / jnp.sum(p, axis=0, keepdims=True))
        o_ref[0, rows, :] = o_t.T.astype(o_ref.dtype)

    def tile(n_ctx_sub):
        for sub in range(tq // ATTN_SUB):
            attend(sub, ctx if sub < n_ctx_sub else s_len)

    pl.when(row0 < ctx)(lambda: tile(ctx // ATTN_SUB))
    pl.when(row0 >= ctx)(lambda: tile(0))


def _attn_call(q, k, v, ctx):
    b, nh, s, _ = q.shape
    tq = _largest_tile(s, (768, 512, 256))
    assert ctx % ATTN_SUB == 0 and ctx <= tq
    return pl.pallas_call(
        functools.partial(_attn_kernel, ctx=ctx),
        grid=(b, nh, s // tq),
        in_specs=[pl.BlockSpec((1, 1, tq, MLA_QK_PAD), lambda i, h, t: (i, h, t, 0)),
                  pl.BlockSpec((1, 1, s, MLA_QK_PAD), lambda i, h, t: (i, h, 0, 0)),
                  pl.BlockSpec((1, 1, MLA_V, s), lambda i, h, t: (i, h, 0, 0))],
        out_specs=pl.BlockSpec((1, tq, MLA_V), lambda i, h, t: (i, t, h)),
        out_shape=jax.ShapeDtypeStruct((b, s, MLA_WIDTH), BF16),
        compiler_params=_params("parallel", "parallel", "arbitrary"),
        name="mla_attention",
    )(q, k, v)


def _padded_row(n, nctx):
    return pl.multiple_of(n * CHUNK + jnp.where(n < nctx, SUBLANES, 2 * SUBLANES), SUBLANES)


def _fill_padded(src_ref, col, pad_scr, *, ctx):
    s_len = src_ref.shape[1]
    width = pad_scr.shape[1]
    nctx = ctx // CHUNK
    zeros = jnp.zeros((SUBLANES, width), F32)
    pad_scr[0:SUBLANES, :] = zeros
    pad_scr[ctx + SUBLANES:ctx + 2 * SUBLANES, :] = zeros
    pad_scr[s_len + 2 * SUBLANES:s_len + 3 * SUBLANES, :] = zeros

    def body(n, carry):
        r = pl.ds(pl.multiple_of(n * CHUNK, CHUNK), CHUNK)
        pad_scr[pl.ds(_padded_row(n, nctx), CHUNK), :] = src_ref[0, r, col].astype(F32)
        return carry

    lax.fori_loop(0, s_len // CHUNK, body, 0)


def _conv4_all(pad_scr, taps_ref, dst_scr, act, *, ctx):
    s_len = dst_scr.shape[0]
    for n in range(s_len // CHUNK):
        base = n * CHUNK + (SUBLANES if n * CHUNK < ctx else 2 * SUBLANES)
        acc = taps_ref[4:5, :] + pad_scr[base:base + CHUNK, :] * taps_ref[1:2, :]
        acc = acc + pad_scr[base - 1:base - 1 + CHUNK, :] * taps_ref[0:1, :]
        acc = acc + pad_scr[base + 1:base + 1 + CHUNK, :] * taps_ref[2:3, :]
        acc = acc + pad_scr[base + 2:base + 2 + CHUNK, :] * taps_ref[3:4, :]
        dst_scr[n * CHUNK:(n + 1) * CHUNK, :] = act(acc)


def _lru_kernel(x_ref, g_ref, w_ref, vec_ref, o_ref, pad_scr, u_scr, af_scr, bf_scr, ab_scr, bb_scr,
                ta_scr, tb_scr, hin_scr, *, ctx):
    s_len = x_ref.shape[1]
    nck = s_len // CHUNK
    ntile, nctile = s_len // SUBLANES, ctx // SUBLANES
    ngrp, ncgrp = ntile // SUBLANES, nctile // SUBLANES
    _fill_padded(x_ref, slice(None), pad_scr, ctx=ctx)
    _conv4_all(pad_scr, vec_ref, u_scr, lambda v: v, ctx=ctx)

    def gates(n, carry):
        r = pl.ds(pl.multiple_of(n * CHUNK, CHUNK), CHUNK)
        u = u_scr[r, :]
        z = _dot(u.astype(BF16), w_ref[0])
        for d, (a_scr, b_scr) in enumerate(((af_scr, bf_scr), (ab_scr, bb_scr))):
            rg = _sigmoid(z[:, (2 * d) * LANES:(2 * d + 1) * LANES] + vec_ref[5 + 2 * d:6 + 2 * d, :])
            ig = _sigmoid(z[:, (2 * d + 1) * LANES:(2 * d + 2) * LANES] + vec_ref[6 + 2 * d:7 + 2 * d, :])
            log_a = -LRU_C * rg * vec_ref[9 + d:10 + d, :]
            a = jnp.exp(log_a)
            a_scr[r, :] = a
            b_scr[r, :] = jnp.sqrt(-jnp.tanh(log_a) * (1.0 + a * a)) * (ig * u)
        return carry

    lax.fori_loop(0, nck, gates, 0, unroll=2)

    part = ntile // 4
    assert part % SUBLANES == 0

    def tile_rows(k, p):
        return pl.ds(p * part * SUBLANES + k, part, stride=SUBLANES)

    for d, (a_scr, b_scr) in enumerate(((af_scr, bf_scr), (ab_scr, bb_scr))):
        ks = range(SUBLANES) if d == 0 else range(SUBLANES - 1, -1, -1)
        for p in range(4):
            pa = qa = None
            for k in ks:
                a_k = a_scr[tile_rows(k, p), :]
                b_k = b_scr[tile_rows(k, p), :]
                if pa is None:
                    pa, qa = a_k, b_k
                else:
                    pa, qa = a_k * pa, a_k * qa + b_k
                    a_scr[tile_rows(k, p), :] = pa
                    b_scr[tile_rows(k, p), :] = qa
            rows = slice(p * part, (p + 1) * part)
            ta_scr[d, rows, :] = pa
            tb_scr[d, rows, :] = qa

    rt = _iota((ntile, LANES), 0) % SUBLANES
    zero = jnp.zeros((SUBLANES, LANES), F32)
    hin_scr[0, 0:SUBLANES, :] = zero
    hin_scr[1, 0:SUBLANES, :] = zero
    for d in range(2):
        pa = ta_scr[d]
        qa = tb_scr[d]
        for sh in (1, 2, 4):
            if d == 0:
                keep = rt >= sh
                a_s, b_s = pltpu.roll(pa, sh, 0), pltpu.roll(qa, sh, 0)
            else:
                keep = rt < SUBLANES - sh
                a_s, b_s = pltpu.roll(pa, ntile - sh, 0), pltpu.roll(qa, ntile - sh, 0)
            qa = qa + pa * jnp.where(keep, b_s, 0.0)
            pa = pa * jnp.where(keep, a_s, 1.0)
        order = range(ngrp) if d == 0 else list(range(ncgrp - 1, -1, -1)) + list(range(ngrp - 1, ncgrp - 1, -1))
        h = zero
        for g in order:
            rows = slice(g * SUBLANES, (g + 1) * SUBLANES)
            t = pa[rows] * h + qa[rows]
            hin_scr[d, SUBLANES + g * SUBLANES:2 * SUBLANES + g * SUBLANES, :] = t
            last = t[SUBLANES - 1:SUBLANES] if d == 0 else t[0:1]
            h = jnp.broadcast_to(last, (SUBLANES, LANES))
    hin_scr[1, SUBLANES + ntile:SUBLANES + ntile + 1, :] = hin_scr[1, SUBLANES:SUBLANES + 1, :]

    tile_id = _iota((part, LANES), 0)
    for d, (a_scr, b_scr) in enumerate(((af_scr, bf_scr), (ab_scr, bb_scr))):
        for p in range(4):
            if d == 0:
                h_in = hin_scr[0, SUBLANES - 1 + p * part:SUBLANES - 1 + (p + 1) * part, :]
            else:
                h_in = hin_scr[1, SUBLANES + 1 + p * part:SUBLANES + 1 + (p + 1) * part, :]
                h_in = jnp.where(tile_id + p * part == nctile - 1, 0.0, h_in)
            for k in range(SUBLANES):
                b_scr[tile_rows(k, p), :] = a_scr[tile_rows(k, p), :] * h_in + b_scr[tile_rows(k, p), :]

    def finish(n, carry):
        r = pl.ds(pl.multiple_of(n * CHUNK, CHUNK), CHUNK)
        g = g_ref[0, r, :].astype(F32)
        o_ref[0, r, :] = ((bf_scr[r, :] + bb_scr[r, :]) * _gelu_tanh(g)).astype(o_ref.dtype)
        return carry

    lax.fori_loop(0, nck, finish, 0)


def _lru_call(p_lru, w_bd, vec, ctx):
    b, s, _ = p_lru.shape
    ncb = LRU_WIDTH // LANES
    return pl.pallas_call(
        functools.partial(_lru_kernel, ctx=ctx),
        grid=(b, ncb),
        in_specs=[pl.BlockSpec((1, s, LANES), lambda i, j: (i, 0, j)),
                  pl.BlockSpec((1, s, LANES), lambda i, j: (i, 0, ncb + j)),
                  pl.BlockSpec((1, LANES, 4 * LANES), lambda i, j: (j, 0, 0)),
                  pl.BlockSpec((16, LANES), lambda i, j: (0, j))],
        out_specs=pl.BlockSpec((1, s, LANES), lambda i, j: (i, 0, j)),
        out_shape=jax.ShapeDtypeStruct((b, s, LRU_WIDTH), BF16),
        scratch_shapes=[pltpu.VMEM((s + 3 * SUBLANES, LANES), F32)] + [pltpu.VMEM((s, LANES), F32)] * 5
        + [pltpu.VMEM((2, s // SUBLANES, LANES), F32)] * 2
        + [pltpu.VMEM((2, s // SUBLANES + 2 * SUBLANES, LANES), F32)],
        compiler_params=_params("parallel", "parallel"),
        name="rg_lru",
    )(p_lru, p_lru, w_bd, vec)


def _expand_heads(m, lane0, rows):
    width = SSD_HG * SSD_HEAD_DIM
    lane = _iota((rows, width), 1)
    out = jnp.broadcast_to(m[:, lane0 + SSD_HG - 1:lane0 + SSD_HG], (rows, width))
    for h in range(SSD_HG - 2, -1, -1):
        out = jnp.where(lane < (h + 1) * SSD_HEAD_DIM, m[:, lane0 + h:lane0 + h + 1], out)
    return out


def _ssd_kernel(x_ref, b_ref, c_ref, z_ref, dt_ref, cvx_ref, cvb_ref, cvc_ref, hp_ref, dsk_ref, o_ref,
                padx, padb, padc, xs_scr, bs_scr, cs_scr, y_scr, mf_scr, mb_scr, ds_scr, cd_scr, st_scr, sf_scr, sb_scr,
                cum_scr, cb_scr, cumt_scr, dtt_scr, *, ctx):
    s_len = x_ref.shape[1]
    nck, nctx = s_len // CHUNK, ctx // CHUNK
    gw = SSD_HG * SSD_HEAD_DIM
    _fill_padded(x_ref, slice(None), padx, ctx=ctx)
    _fill_padded(b_ref, slice(None), padb, ctx=ctx)
    _fill_padded(c_ref, slice(None), padc, ctx=ctx)

    _conv4_all(padx, cvx_ref, xs_scr, _silu, ctx=ctx)
    _conv4_all(padb, cvb_ref, bs_scr, _silu, ctx=ctx)
    _conv4_all(padc, cvc_ref, cs_scr, _silu, ctx=ctx)

    ii = _iota((CHUNK, CHUNK), 0)
    jj = _iota((CHUNK, CHUNK), 1)
    spread = (_iota((LANES, 2 * gw), 0) == _iota((LANES, 2 * gw), 1) // SSD_HEAD_DIM).astype(BF16)

    lane256 = _iota((CHUNK, gw), 1)
    dt_bias = hp_ref[0, 0:1, :]
    a_neg = hp_ref[0, 1:2, :]

    def rows(n):
        return pl.ds(pl.multiple_of(n * CHUNK, CHUNK), CHUNK)

    pair = 2 * CHUNK
    assert nck % 2 == 0
    i2 = _iota((pair, pair), 0)
    j2 = _iota((pair, pair), 1)
    tri2 = ((j2 <= i2) & (j2 // CHUNK == i2 // CHUNK)).astype(BF16)
    first_chunk = _iota((pair, LANES), 0) < CHUNK
    is_fwd_lane = _iota((pair, LANES), 1) < SSD_HG

    def decays(i, carry):
        r2 = pl.ds(pl.multiple_of(i * pair, pair), pair)
        dt = _softplus(dt_ref[0, r2, :] + dt_bias)
        dta = dt * a_neg
        af = _dot2_l(tri2, dta)
        tot = jnp.where(first_chunk, af[CHUNK - 1:CHUNK, :], af[pair - 1:pair, :])
        rv = tot - af + dta
        cum = jnp.where(is_fwd_lane, af, rv)
        cum_scr[r2, :] = cum
        cum_t = cum.T
        dt_t = dt.T
        mult = _dot2_r(jnp.exp(cum), spread)
        mf_scr[r2, :] = mult[:, 0:gw]
        mb_scr[r2, :] = mult[:, gw:2 * gw]
        edge = jnp.where(is_fwd_lane, tot, jnp.where(first_chunk, rv[0:1, :], rv[CHUNK:CHUNK + 1, :]))
        wgt = _dot((jnp.exp(edge - cum) * dt).astype(BF16), spread)
        xg = xs_scr[r2, :]
        xw_f = (xg * wgt[:, 0:gw]).astype(BF16)
        xw_b = (xg * wgt[:, gw:2 * gw]).astype(BF16)
        chunk_decay = jnp.exp(edge)
        for c in range(2):
            n = 2 * i + c
            r = rows(n)
            rc = slice(c * CHUNK, (c + 1) * CHUNK)
            cumt_scr[n] = cum_t[0:SUBLANES, rc]
            dtt_scr[n] = dt_t[0:SUBLANES, rc]
            bsf = bs_scr[r, :]
            cb_scr[r, :] = _dot_nt(cs_scr[r, :].astype(BF16), bsf.astype(BF16))
            bt = bsf.T.astype(BF16)
            ds_scr[n, :, 0:gw] = _dot(bt, xw_f[rc])
            ds_scr[n, :, gw:2 * gw] = _dot(bt, xw_b[rc])
            cd_scr[n] = jnp.broadcast_to(chunk_decay[c * CHUNK:c * CHUNK + 1, :], (SUBLANES, LANES))
        return carry

    lax.fori_loop(0, nck // 2, decays, 0)

    def intra(n, carry):
        r = rows(n)
        cum = cum_scr[r, :]
        cum_t = cumt_scr[n]
        dt_t = dtt_scr[n]
        cb = cb_scr[r, :]
        xg = xs_scr[r, :]
        y = jnp.zeros((CHUNK, gw), F32)
        for h in range(SSD_HG):
            hb = SSD_HG + h
            arg = jnp.where(ii >= jj, cum[:, h:h + 1] - cum_t[h:h + 1, :], cum[:, hb:hb + 1] - cum_t[hb:hb + 1, :])
            wdt = jnp.where(ii > jj, dt_t[h:h + 1, :],
                            jnp.where(ii < jj, dt_t[hb:hb + 1, :], dt_t[h:h + 1, :] + dt_t[hb:hb + 1, :]))
            w = (cb * jnp.exp(arg) * wdt).astype(BF16)
            in_head = (lane256 >= h * SSD_HEAD_DIM) & (lane256 < (h + 1) * SSD_HEAD_DIM)
            y = y + _dot(w, jnp.where(in_head, xg, 0.0).astype(BF16))
        y_scr[r, :] = y
        return carry

    lax.fori_loop(0, nck, intra, 0, unroll=2)

    sf_scr[...] = jnp.zeros(sf_scr.shape, F32)
    sb_scr[...] = jnp.zeros(sb_scr.shape, F32)

    def states(i, carry):
        sf = sf_scr[...]
        st_scr[i, :, 0:gw] = sf.astype(BF16)
        sf_scr[...] = sf * _expand_heads(cd_scr[i, 0:1, :], 0, 1) + ds_scr[i, :, 0:gw]
        j = jnp.where(i < nctx, nctx - 1 - i, nck - 1 - (i - nctx))
        sb = sb_scr[...]
        st_scr[j, :, gw:2 * gw] = sb.astype(BF16)
        sb_scr[...] = sb * _expand_heads(cd_scr[j, 0:1, :], SSD_HG, 1) + ds_scr[j, :, gw:2 * gw]
        return carry

    lax.fori_loop(0, nck, states, 0)

    def inter(n, carry):
        r = rows(n)
        ys = _dot(cs_scr[r, :].astype(BF16), st_scr[n])
        y = y_scr[r, :] + ys[:, 0:gw] * mf_scr[r, :] + ys[:, gw:2 * gw] * mb_scr[r, :] + dsk_ref[...] * xs_scr[r, :]
        o_ref[0, r, :] = (y * _silu(z_ref[0, r, :].astype(F32))).astype(o_ref.dtype)
        return carry

    lax.fori_loop(0, nck, inter, 0, unroll=2)


def _ssd_call(p_ssd, dt, cv, hp, dsk, ctx):
    b, s, _ = p_ssd.shape
    gw = SSD_HG * SSD_HEAD_DIM
    nx = SSD_WIDTH // LANES
    nbc = SSD_GROUPS * SSD_STATE // LANES
    pad = lambda w: pltpu.VMEM((s + 3 * SUBLANES, w), F32)
    return pl.pallas_call(
        functools.partial(_ssd_kernel, ctx=ctx),
        grid=(b, SSD_GROUPS),
        in_specs=[pl.BlockSpec((1, s, gw), lambda i, g: (i, 0, g)),
                  pl.BlockSpec((1, s, LANES), lambda i, g: (i, 0, nx + g)),
                  pl.BlockSpec((1, s, LANES), lambda i, g: (i, 0, nx + nbc + g)),
                  pl.BlockSpec((1, s, gw), lambda i, g: (i, 0, SSD_XBC // gw + g)),
                  pl.BlockSpec((1, s, LANES), lambda i, g: (i, 0, g)),
                  pl.BlockSpec((8, gw), lambda i, g: (0, g)),
                  pl.BlockSpec((8, LANES), lambda i, g: (0, nx + g)),
                  pl.BlockSpec((8, LANES), lambda i, g: (0, nx + nbc + g)),
                  pl.BlockSpec((1, 8, LANES), lambda i, g: (g, 0, 0)),
                  pl.BlockSpec((1, gw), lambda i, g: (0, g))],
        out_specs=pl.BlockSpec((1, s, gw), lambda i, g: (i, 0, g)),
        out_shape=jax.ShapeDtypeStruct((b, s, SSD_WIDTH), BF16),
        scratch_shapes=[pad(gw), pad(LANES), pad(LANES),
                        pltpu.VMEM((s, gw), F32), pltpu.VMEM((s, LANES), F32), pltpu.VMEM((s, LANES), F32),
                        pltpu.VMEM((s, gw), F32), pltpu.VMEM((s, gw), F32), pltpu.VMEM((s, gw), F32),
                        pltpu.VMEM((s // CHUNK, SSD_STATE, 2 * gw), F32), pltpu.VMEM((s // CHUNK, SUBLANES, LANES), F32),
                        pltpu.VMEM((s // CHUNK, SSD_STATE, 2 * gw), BF16),
                        pltpu.VMEM((SSD_STATE, gw), F32), pltpu.VMEM((SSD_STATE, gw), F32),
                        pltpu.VMEM((s, LANES), F32), pltpu.VMEM((s, LANES), F32),
                        pltpu.VMEM((s // CHUNK, SUBLANES, LANES), F32), pltpu.VMEM((s // CHUNK, SUBLANES, LANES), F32)],
        compiler_params=_params("parallel", "parallel"),
        name="ssd",
    )(p_ssd, p_ssd, p_ssd, p_ssd, dt, cv, cv, cv, hp, dsk)


def _merge_kernel(g_ref, ret_ref, mla_ref, lru_ref, ssd_ref, h_ref, mx_ref, mc_ref, wb_ref, wo_ref, vec_ref, nw_ref,
                  o_ref, *, ctx, first_tile, alpha):
    tm = h_ref.shape[1]
    is_ctx = (pl.program_id(1) + first_tile) * tm + _iota((tm, 1), 0) < ctx
    ssd = (_rms(ssd_ref[0].astype(F32)) * nw_ref[...]).astype(BF16)
    acc = None
    for i, br in enumerate((ret_ref[0], mla_ref[0], lru_ref[0], ssd)):
        gate = _sigmoid(g_ref[0, :, i * D_MODEL:(i + 1) * D_MODEL].astype(F32))
        term = gate * _dot(br, wb_ref[i])
        acc = term if acc is None else acc + term
    o = _dot(acc.astype(BF16), wo_ref[...])
    y = _ln(alpha * h_ref[0] + _pick_mod(is_ctx, mx_ref, mc_ref, 2) * o)
    o_ref[0] = y * vec_ref[0:1, :] + vec_ref[1:2, :]


def _merge_call(p_gates, y_ret, y_mla, y_lru, y_ssd, h, mods, wb, wo, vec, nw, ctx, with_ctx, alpha):
    b, s, _ = h.shape
    tm = _largest_tile(s, (768, 512, 256)) if with_ctx else ROW_TILE
    first = 0 if with_ctx else ctx // tm
    nb = mods.shape[0] - 1
    tok = lambda w: pl.BlockSpec((1, tm, w), lambda i, t: (i, t + first, 0))
    return pl.pallas_call(
        functools.partial(_merge_kernel, ctx=ctx, first_tile=first, alpha=alpha),
        grid=(b, s // tm - first),
        in_specs=[tok(SEG_GATES), tok(RET_WIDTH), tok(MLA_WIDTH), tok(LRU_WIDTH), tok(SSD_WIDTH), tok(D_MODEL),
                  pl.BlockSpec((1, 8, D_MODEL), lambda i, t: (i, 0, 0)),
                  pl.BlockSpec((1, 8, D_MODEL), lambda i, t: (nb, 0, 0)),
                  _resident(wb.shape), _resident(wo.shape), _resident(vec.shape), _resident(nw.shape)],
        out_specs=pl.BlockSpec((1, tm, D_MODEL), lambda i, t: (i, t, 0)),
        out_shape=jax.ShapeDtypeStruct((b, s - first * tm, D_MODEL), F32),
        compiler_params=_params("parallel", "parallel"),
        name="merge_postnorm",
    )(p_gates, y_ret, y_mla, y_lru, y_ssd, h, mods, mods, wb, wo, vec, nw)


def _ffn_kernel(h_ref, hp_ref, hn_ref, mx_ref, mc_ref, wu_ref, wd_ref, cv_ref, vec_ref, o_ref, *, ctx, s_len, alpha):
    tm = h_ref.shape[1]
    t = pl.program_id(1)
    row0 = t * tm
    h = h_ref[0]
    in_ctx = row0 + _iota((tm, 1), 0) < ctx

    def pick(row, is_ctx):
        if ctx == 0:
            return mx_ref[0, row:row + 1, :]
        return jnp.where(is_ctx, mc_ref[0, row:row + 1, :], mx_ref[0, row:row + 1, :])

    def mod(x, is_ctx):
        return _ln(x) * (1.0 + pick(4, is_ctx)) + pick(3, is_ctx)

    prev_ok = jnp.logical_and(row0 != 0, row0 != ctx)
    next_ok = jnp.logical_and(row0 + tm != ctx, row0 + tm != s_len)
    xm = jnp.concatenate([jnp.where(prev_ok, mod(hp_ref[0], row0 - 1 < ctx), 0.0), mod(h, in_ctx),
                          jnp.where(next_ok, mod(hn_ref[0], row0 + tm < ctx), 0.0)], axis=0).astype(BF16)
    rows = tm + 2 * SUBLANES
    lo, hi = SUBLANES, SUBLANES + tm
    seam = ctx % tm
    seam_tile = ctx // tm
    seam_row = _iota((2 * SUBLANES, 1), 0)

    def conv(u, c0):
        w_prev, w_mid, w_next = (cv_ref[k:k + 1, c0:c0 + FF_CHUNK] for k in range(3))
        out = cv_ref[3:4, c0:c0 + FF_CHUNK] + u[lo:hi] * w_mid
        out = out + pltpu.roll(u, 1, 0)[lo:hi] * w_prev
        out = out + pltpu.roll(u, rows - 1, 0)[lo:hi] * w_next
        if seam:
            slab = u[seam:seam + 2 * SUBLANES]
            fix = (jnp.where(seam_row == SUBLANES - 1, pltpu.roll(slab, 2 * SUBLANES - 1, 0) * w_next, 0.0)
                   + jnp.where(seam_row == SUBLANES, pltpu.roll(slab, 1, 0) * w_prev, 0.0))
            fix = jnp.where(t == seam_tile, fix, 0.0)
            out = jnp.concatenate([out[:seam - SUBLANES], out[seam - SUBLANES:seam + SUBLANES] - fix,
                                   out[seam + SUBLANES:]], axis=0)
        return out

    acc = jnp.zeros((tm, D_MODEL), F32)
    for c0 in range(0, D_FF, FF_CHUNK):
        g = conv(_dot(xm, wu_ref[:, c0:c0 + FF_CHUNK]), c0)
        v = conv(_dot(xm, wu_ref[:, D_FF + c0:D_FF + c0 + FF_CHUNK]), D_FF + c0)
        acc = acc + _dot((_silu(g) * v).astype(BF16), wd_ref[c0:c0 + FF_CHUNK, :])
    y = _ln(alpha * h + pick(5, in_ctx) * acc)
    o_ref[0] = y * vec_ref[0:1, :] + vec_ref[1:2, :]


def _ffn_call(h, mods, wu, wd, cv, vec, ctx, alpha):
    b, s, _ = h.shape
    tm = _largest_tile(s, (768, 512, 256))
    assert ctx % SUBLANES == 0 and (ctx % tm == 0 or SUBLANES <= ctx % tm <= tm - SUBLANES)
    nb = mods.shape[0] - 1
    per = tm // SUBLANES
    hi_blk = s // SUBLANES - 1
    return pl.pallas_call(
        functools.partial(_ffn_kernel, ctx=ctx, s_len=s, alpha=alpha),
        grid=(b, s // tm),
        in_specs=[pl.BlockSpec((1, tm, D_MODEL), lambda i, t: (i, t, 0)),
                  pl.BlockSpec((1, SUBLANES, D_MODEL), lambda i, t: (i, jnp.maximum(t * per - 1, 0), 0)),
                  pl.BlockSpec((1, SUBLANES, D_MODEL), lambda i, t: (i, jnp.minimum((t + 1) * per, hi_blk), 0)),
                  pl.BlockSpec((1, 8, D_MODEL), lambda i, t: (i, 0, 0)),
                  pl.BlockSpec((1, 8, D_MODEL), lambda i, t: (nb, 0, 0)),
                  _resident(wu.shape), _resident(wd.shape), _resident(cv.shape), _resident(vec.shape)],
        out_specs=pl.BlockSpec((1, tm, D_MODEL), lambda i, t: (i, t, 0)),
        out_shape=jax.ShapeDtypeStruct((b, s, D_MODEL), F32),
        compiler_params=_params("parallel", "parallel"),
        name="conv_ffn_postnorm",
    )(h, h, h, mods, mods, wu, wd, cv, vec)


def _pad_rows(a, rows):
    return jnp.pad(a, ((0, rows - a.shape[0]),) + ((0, 0),) * (a.ndim - 1))


def _rope_tables(t_len, ctx):
    rows = t_len // GRID_W
    r, col = jnp.meshgrid(jnp.arange(rows, dtype=F32), jnp.arange(GRID_W, dtype=F32), indexing='ij')

    def tables(dim):
        quarter = dim // 4
        inv = ROPE_BASE ** (-jnp.arange(quarter, dtype=F32) / quarter)
        ang = jnp.concatenate([r.reshape(-1, 1) * inv, col.reshape(-1, 1) * inv], axis=-1)
        cos, sin = jnp.cos(ang), jnp.sin(ang)
        return jnp.concatenate([cos, cos], -1), jnp.concatenate([-sin, sin], -1)

    def with_ctx(cos, sin, width):
        cos = jnp.pad(cos, ((0, 0), (0, width - cos.shape[1])), constant_values=1.0)
        sin = jnp.pad(sin, ((0, 0), (0, width - sin.shape[1])))
        return (jnp.concatenate([jnp.ones((ctx, width), F32), cos], 0),
                jnp.concatenate([jnp.zeros((ctx, width), F32), sin], 0))

    return with_ctx(*tables(RET_HEAD_DIM), RET_HEAD_DIM), with_ctx(*tables(MLA_ROPE), LANES)


def _layer_params(l, p):
    w_in = p['w_in'][l]
    offs = np.cumsum((0, SEG_GATES, RET_WIDTH, RET_WIDTH, RET_WIDTH, RET_WIDTH, MLA_Q_RANK, MLA_KV_RANK, MLA_ROPE,
                      LRU_WIDTH, LRU_WIDTH, SSD_WIDTH, SSD_XBC, 2 * SSD_HEADS))
    piece = lambda i: w_in[:, offs[i]:offs[i + 1]]
    zeros = lambda n: jnp.zeros((D_MODEL, n), w_in.dtype)
    w_cat = jnp.concatenate([piece(0), piece(1), piece(2), piece(3), piece(4), piece(11), piece(10), piece(8), piece(9),
                             piece(5), piece(6), piece(7), zeros(SEG_MLA - MLA_Q_RANK - MLA_KV_RANK - MLA_ROPE)], axis=1)
    dt_w = piece(12).reshape(D_MODEL, 2, SSD_GROUPS, SSD_HG).transpose(0, 2, 1, 3).reshape(D_MODEL, SSD_GROUPS, 2 * SSD_HG)
    w_dt = jnp.pad(dt_w, ((0, 0), (0, 0), (0, LANES - 2 * SSD_HG))).reshape(D_MODEL, DT_COLS)

    def per_group(v):
        v = v.reshape(2, SSD_GROUPS, SSD_HG).transpose(1, 0, 2).reshape(SSD_GROUPS, 2 * SSD_HG)
        return jnp.pad(v, ((0, 0), (0, LANES - 2 * SSD_HG)))

    ssd_hp = jnp.stack([per_group(p['ssd_dt_bias'][l]), per_group(-jnp.exp(p['ssd_a_log'][l].astype(F32)))], axis=1)
    ssd_hp = jnp.pad(ssd_hp, ((0, 0), (0, 6), (0, 0)))

    wq = p['mla_w_uq'][l].reshape(MLA_Q_RANK, MLA_HEADS, MLA_NOPE + MLA_ROPE)
    wq = jnp.pad(wq, ((0, 0), (0, 0), (0, MLA_QK_PAD - MLA_NOPE - MLA_ROPE))).reshape(MLA_Q_RANK, MLA_HEADS * MLA_QK_PAD)

    wkv = p['mla_w_ukv'][l].reshape(MLA_KV_RANK, MLA_HEADS, MLA_NOPE + MLA_V)

    gw = p['lru_gate_w'][l].reshape(4, LRU_WIDTH // LANES, 2, LRU_BLOCK, LRU_BLOCK)
    w_bd = jnp.einsum('ajpcd,pq->japcqd', gw, jnp.eye(2, dtype=gw.dtype)).reshape(LRU_WIDTH // LANES, 4, LANES, LANES)
    w_bd = w_bd.transpose(0, 2, 1, 3).reshape(LRU_WIDTH // LANES, LANES, 4 * LANES)
    lru_vec = jnp.concatenate([p['lru_conv_w'][l], p['lru_conv_b'][l][None], p['lru_gate_b'][l].reshape(4, LRU_WIDTH),
                               jax.nn.softplus(-p['lru_lambda'][l].astype(F32))], axis=0)

    log_g = jax.nn.log_sigmoid(p['ret_decay'][l].astype(F32))
    ret_lg = jnp.broadcast_to(_pad_rows(log_g, 8).T[:, :, None], (RET_HEADS, 8, RET_HEAD_DIM))

    return dict(
        ada_w=p['ada_w'][l].astype(BF16), ada_b=p['ada_b'][l][None],
        w_cat=w_cat.astype(BF16), w_dt=w_dt.astype(BF16),
        ret_lg=ret_lg, ret_gw=p['ret_gn_w'][l].reshape(RET_HEADS, 1, RET_HEAD_DIM),
        ret_gb=p['ret_gn_b'][l].reshape(RET_HEADS, 1, RET_HEAD_DIM),
        mla_qn=p['mla_q_norm'][l][None], mla_kn=p['mla_kv_norm'][l][None],
        mla_wq=wq.astype(BF16), mla_wk=wkv[:, :, :MLA_NOPE].reshape(MLA_KV_RANK, MLA_HEADS * MLA_NOPE).astype(BF16),
        mla_wvt=wkv[:, :, MLA_NOPE:].reshape(MLA_KV_RANK, MLA_HEADS * MLA_V).T.astype(BF16),
        lru_w=w_bd.astype(BF16), lru_vec=_pad_rows(lru_vec, 16),
        ssd_cv=_pad_rows(jnp.concatenate([p['ssd_conv_w'][l], p['ssd_conv_b'][l][None]], 0), 8),
        ssd_hp=ssd_hp, ssd_dsk=jnp.repeat(p['ssd_d'][l].astype(F32), SSD_HEAD_DIM)[None],
        ssd_nw=p['ssd_norm_w'][l][None],
        w_branch=p['w_branch'][l].astype(BF16), w_out=p['w_out'][l].astype(BF16),
        ln1=_pad_rows(jnp.stack([p['ln1_w'][l], p['ln1_b'][l]]), 8),
        ffn_wu=p['ffn_w_up'][l].astype(BF16), ffn_wd=p['ffn_w_down'][l].astype(BF16),
        ffn_cv=_pad_rows(jnp.concatenate([p['ffn_conv_w'][l], p['ffn_conv_b'][l][None]], 0), 8),
        ln2=_pad_rows(jnp.stack([p['ln2_w'][l], p['ln2_b'][l]]), 8),
    )


def kernel(x, c, ctx, c_ctx, ada_w, ada_b, w_in, ret_decay, ret_gn_w, ret_gn_b, mla_q_norm, mla_w_uq, mla_kv_norm, mla_w_ukv, lru_conv_w, lru_conv_b, lru_gate_w, lru_gate_b, lru_lambda, ssd_conv_w, ssd_conv_b, ssd_dt_bias, ssd_a_log, ssd_d, ssd_norm_w, w_branch, w_out, ln1_w, ln1_b, ffn_w_up, ffn_conv_w, ffn_conv_b, ffn_w_down, ln2_w, ln2_b):
    p = dict(ada_w=ada_w, ada_b=ada_b, w_in=w_in, ret_decay=ret_decay, ret_gn_w=ret_gn_w, ret_gn_b=ret_gn_b,
             mla_q_norm=mla_q_norm, mla_w_uq=mla_w_uq, mla_kv_norm=mla_kv_norm, mla_w_ukv=mla_w_ukv,
             lru_conv_w=lru_conv_w, lru_conv_b=lru_conv_b, lru_gate_w=lru_gate_w, lru_gate_b=lru_gate_b,
             lru_lambda=lru_lambda, ssd_conv_w=ssd_conv_w, ssd_conv_b=ssd_conv_b, ssd_dt_bias=ssd_dt_bias,
             ssd_a_log=ssd_a_log, ssd_d=ssd_d, ssd_norm_w=ssd_norm_w, w_branch=w_branch, w_out=w_out,
             ln1_w=ln1_w, ln1_b=ln1_b, ffn_w_up=ffn_w_up, ffn_conv_w=ffn_conv_w, ffn_conv_b=ffn_conv_b,
             ffn_w_down=ffn_w_down, ln2_w=ln2_w, ln2_b=ln2_b)
    batch, t_len, _ = x.shape
    n_ctx = ctx.shape[1]
    depth = ada_w.shape[0]
    assert n_ctx % ROW_TILE == 0 and t_len % ROW_TILE == 0 and t_len % GRID_W == 0
    alpha = (2 * depth) ** 0.25
    (ret_cos, ret_sin), (mla_cos, mla_sin) = _rope_tables(t_len, n_ctx)
    cond = _pad_rows(jnp.concatenate([c, c_ctx[None]], axis=0), -(-(batch + 1) // 8) * 8)
    h = jnp.concatenate([ctx, x], axis=1)
    for l in range(depth):
        with_ctx = l < depth - 1
        w = _layer_params(l, p)
        mods = _mod_call(cond, w['ada_w'], w['ada_b'])[:batch + 1].reshape(batch + 1, 6, D_MODEL)
        mods = jnp.pad(mods, ((0, 0), (0, 2), (0, 0)))
        p_gates, p_ret, p_ssd, p_lru, p_mla, p_dt = _inproj_call(h, mods, w['w_cat'], w['w_dt'], n_ctx)
        y_ret = _ret_call(p_ret, ret_cos, ret_sin, w['ret_lg'], w['ret_gw'], w['ret_gb'], n_ctx)
        q, k, v = _mla_prep_call(p_mla, w['mla_qn'], w['mla_kn'], w['mla_wq'], w['mla_wk'], w['mla_wvt'], mla_cos, mla_sin)
        y_mla = _attn_call(q, k, v, n_ctx)
        y_lru = _lru_call(p_lru, w['lru_w'], w['lru_vec'], n_ctx)
        y_ssd = _ssd_call(p_ssd, p_dt, w['ssd_cv'], w['ssd_hp'], w['ssd_dsk'], n_ctx)
        h1 = _merge_call(p_gates, y_ret, y_mla, y_lru, y_ssd, h, mods, w['w_branch'], w['w_out'], w['ln1'], w['ssd_nw'],
                         n_ctx, with_ctx, alpha)
        h = _ffn_call(h1, mods, w['ffn_wu'], w['ffn_wd'], w['ffn_cv'], w['ln2'], n_ctx if with_ctx else 0, alpha)
    return h
```

```python
import functools
import math

import jax
import jax.numpy as jnp
import numpy as np
from jax import lax
from jax.experimental import pallas as pl
from jax.experimental.pallas import tpu as pltpu

F32 = jnp.float32
BF16 = jnp.bfloat16

D_MODEL = 1024
GRID_W = 64
ROPE_BASE = 10000.0
LN_EPS = 1e-6
RMS_EPS = 1e-6

RET_HEADS = 4
RET_HEAD_DIM = 128
RET_WIDTH = RET_HEADS * RET_HEAD_DIM

MLA_HEADS = 4
MLA_Q_RANK = 384
MLA_KV_RANK = 256
MLA_NOPE = 128
MLA_ROPE = 64
MLA_V = 128
MLA_WIDTH = MLA_HEADS * MLA_V
MLA_QK_PAD = 256

LRU_WIDTH = 512
LRU_BLOCKS = 8
LRU_BLOCK = LRU_WIDTH // LRU_BLOCKS
LRU_C = 8.0

SSD_HEADS = 8
SSD_HEAD_DIM = 64
SSD_WIDTH = SSD_HEADS * SSD_HEAD_DIM
SSD_GROUPS = 2
SSD_STATE = 128
SSD_XBC = SSD_WIDTH + 2 * SSD_GROUPS * SSD_STATE
SSD_HG = SSD_HEADS // SSD_GROUPS

N_BRANCH = 4
D_FF = 2816

CHUNK = 128
LANES = 128
SUBLANES = 8
ROW_TILE = 256
ATTN_SUB = 256
N_CHUNK = 512
FF_CHUNK = 256
VMEM_LIMIT = 56 * 1024 * 1024

SEG_GATES = N_BRANCH * D_MODEL
SEG_RET = 4 * RET_WIDTH
SEG_SSD = SSD_XBC + SSD_WIDTH
SEG_LRU = 2 * LRU_WIDTH
SEG_MLA = 768
SEGS = (SEG_GATES, SEG_RET, SEG_SSD, SEG_LRU, SEG_MLA)
DT_COLS = SSD_GROUPS * LANES


def _dot(a, b):
    return jnp.dot(a, b, preferred_element_type=F32)


def _dot_nt(a, b):
    return lax.dot_general(a, b, (((1,), (1,)), ((), ())), preferred_element_type=F32)


def _split2(x):
    hi = x.astype(BF16)
    return hi, (x - hi.astype(F32)).astype(BF16)


def _dot2_l(m, x):
    hi, lo = _split2(x)
    return _dot(m, hi) + _dot(m, lo)


def _dot2_r(x, m):
    hi, lo = _split2(x)
    return _dot(hi, m) + _dot(lo, m)


def _ln(x):
    mu = jnp.mean(x, axis=-1, keepdims=True)
    xc = x - mu
    var = jnp.mean(xc * xc, axis=-1, keepdims=True)
    return xc * lax.rsqrt(var + LN_EPS)


def _rms(x):
    return x * lax.rsqrt(jnp.mean(x * x, axis=-1, keepdims=True) + RMS_EPS)


def _sigmoid(x):
    return 0.5 * jnp.tanh(0.5 * x) + 0.5


def _silu(x):
    return x * _sigmoid(x)


def _softplus(x):
    return jnp.maximum(x, 0.0) + jnp.log1p(jnp.exp(-jnp.abs(x)))


def _gelu_tanh(x):
    return 0.5 * x * (1.0 + jnp.tanh(math.sqrt(2.0 / math.pi) * (x + 0.044715 * (x * x * x))))


def _iota(shape, dim):
    return lax.broadcasted_iota(jnp.int32, shape, dim)


def _largest_tile(n, candidates):
    return next(t for t in candidates if n % t == 0)


def _resident(shape):
    nd = len(shape)
    return pl.BlockSpec(shape, lambda *_: (0,) * nd, pipeline_mode=pl.Buffered(1))


def _params(*sem):
    return pltpu.CompilerParams(dimension_semantics=sem, vmem_limit_bytes=VMEM_LIMIT)


def _mod_kernel(c_ref, w_ref, b_ref, o_ref):
    s = _silu(c_ref[...])
    o_ref[...] = _dot(s.astype(BF16), w_ref[...]) + b_ref[...]


def _mod_call(cc, w, b):
    rows = cc.shape[0]
    n = w.shape[1]
    return pl.pallas_call(
        _mod_kernel,
        grid=(n // D_MODEL,),
        in_specs=[
            pl.BlockSpec((rows, D_MODEL), lambda j: (0, 0)),
            pl.BlockSpec((D_MODEL, D_MODEL), lambda j: (0, j)),
            pl.BlockSpec((1, D_MODEL), lambda j: (0, j)),
        ],
        out_specs=pl.BlockSpec((rows, D_MODEL), lambda j: (0, j)),
        out_shape=jax.ShapeDtypeStruct((rows, n), F32),
        compiler_params=_params("arbitrary"),
        name="adaln_mod",
    )(cc, w, b)


def _pick_mod(is_ctx, mx_ref, mc_ref, row):
    return jnp.where(is_ctx, mc_ref[0, row:row + 1, :], mx_ref[0, row:row + 1, :])


def _inproj_kernel(h_ref, mx_ref, mc_ref, w_ref, wdt_ref, og, orr, os_, ol, om, odt, *, ctx):
    tm = h_ref.shape[1]
    is_ctx = pl.program_id(1) * tm < ctx
    shift = _pick_mod(is_ctx, mx_ref, mc_ref, 0)
    scale = _pick_mod(is_ctx, mx_ref, mc_ref, 1)
    xm = (_ln(h_ref[0]) * (1.0 + scale) + shift).astype(BF16)
    col = 0
    for oref, width in zip((og, orr, os_, ol, om), SEGS):
        for c0 in range(0, width, N_CHUNK):
            cw = min(N_CHUNK, width - c0)
            oref[0, :, c0:c0 + cw] = _dot(xm, w_ref[:, col + c0:col + c0 + cw]).astype(oref.dtype)
        col += width
    odt[0] = _dot(xm, wdt_ref[...])


def _inproj_call(h, mods, w_cat, w_dt, ctx):
    b, s, _ = h.shape
    tm = ROW_TILE
    nb = mods.shape[0] - 1
    tok = lambda w: pl.BlockSpec((1, tm, w), lambda i, t: (i, t, 0))
    outs = [jax.ShapeDtypeStruct((b, s, w), BF16) for w in SEGS] + [jax.ShapeDtypeStruct((b, s, DT_COLS), F32)]
    return pl.pallas_call(
        functools.partial(_inproj_kernel, ctx=ctx),
        grid=(b, s // tm),
        in_specs=[
            tok(D_MODEL),
            pl.BlockSpec((1, 8, D_MODEL), lambda i, t: (i, 0, 0)),
            pl.BlockSpec((1, 8, D_MODEL), lambda i, t: (nb, 0, 0)),
            _resident(w_cat.shape),
            _resident(w_dt.shape),
        ],
        out_specs=[tok(w) for w in SEGS] + [tok(DT_COLS)],
        out_shape=outs,
        compiler_params=_params("parallel", "parallel"),
        name="in_proj",
    )(h, mods, mods, w_cat, w_dt)


def _ret_kernel(q_ref, k_ref, v_ref, g_ref, cos_ref, sin_ref, lg_ref, gw_ref, gb_ref, o_ref,
                y_scr, q_scr, kv_scr, st_scr, *, ctx):
    s_len = q_ref.shape[1]
    nck, nctx = s_len // CHUNK, ctx // CHUNK
    hd = RET_HEAD_DIM
    scale = hd ** -0.5
    lgf = lg_ref[0, 0:1, :]
    lgb = lg_ref[0, 1:2, :]
    d = (_iota((CHUNK, CHUNK), 0) - _iota((CHUNK, CHUNK), 1)).astype(F32)
    dec = (jnp.where(d >= 0, jnp.exp(jnp.maximum(d, 0.0) * lgf), 0.0)
           + jnp.where(d <= 0, jnp.exp(jnp.maximum(-d, 0.0) * lgb), 0.0))
    pos = _iota((CHUNK, 1), 0).astype(F32)
    posl = _iota((1, CHUNK), 1).astype(F32)
    qdec_f = jnp.exp((pos + 1.0) * lgf)
    qdec_b = jnp.exp((CHUNK - pos) * lgb)
    kdec_f = jnp.exp((CHUNK - 1.0 - posl) * lgf)
    kdec_b = jnp.exp(posl * lgb)
    cdec_f = jnp.exp(CHUNK * lgf)
    cdec_b = jnp.exp(CHUNK * lgb)
    gw = gw_ref[0]
    gb = gb_ref[0]

    def rows(n):
        if isinstance(n, int):
            return pl.ds(n * CHUNK, CHUNK)
        return pl.ds(pl.multiple_of(n * CHUNK, CHUNK), CHUNK)

    def intra(n, carry):
        r = rows(n)
        cos = cos_ref[r, :]
        sin = sin_ref[r, :]
        q = q_ref[0, r, :].astype(F32)
        k = k_ref[0, r, :].astype(F32)
        v = v_ref[0, r, :]
        qb = (q * cos + pltpu.roll(q, hd // 2, 1) * sin).astype(BF16)
        kr = (k * cos + pltpu.roll(k, hd // 2, 1) * sin) * scale
        q_scr[r, :] = qb
        sc = _dot_nt(qb, kr.astype(BF16)) * dec
        y_scr[r, :] = _dot(sc.astype(BF16), v)
        kt = kr.T
        kv_scr[n] = _dot(jnp.concatenate([kt * kdec_f, kt * kdec_b], axis=0).astype(BF16), v)
        return carry

    for n in range(nck):
        intra(n, 0)

    def states(i, sts):
        sf, sb = sts
        st_scr[i, 0:hd, :] = sf.astype(BF16)
        sf = sf * cdec_f + kv_scr[i, 0:hd, :]
        j = jnp.where(i < nctx, nctx - 1 - i, nck - 1 - (i - nctx))
        st_scr[j, hd:2 * hd, :] = sb.astype(BF16)
        sb = sb * cdec_b + kv_scr[j, hd:2 * hd, :]
        return sf, sb

    zero = jnp.zeros((hd, hd), F32)
    lax.fori_loop(0, nck, states, (zero, zero))

    def inter(n, carry):
        r = rows(n)
        q = q_scr[r, :].astype(F32)
        qq = jnp.concatenate([q * qdec_f, q * qdec_b], axis=1).astype(BF16)
        y = y_scr[r, :] + _dot(qq, st_scr[n])
        g = g_ref[0, r, :].astype(F32)
        o_ref[0, r, :] = (_silu(g) * (_ln(y) * gw + gb)).astype(o_ref.dtype)
        return carry

    lax.fori_loop(0, nck, inter, 0, unroll=6)


def _ret_call(p_ret, cos, sin, lg, gw, gb, ctx):
    b, s, _ = p_ret.shape
    col = lambda off: pl.BlockSpec((1, s, RET_HEAD_DIM), lambda i, h: (i, 0, off + h))
    vec = pl.BlockSpec((1, 1, RET_HEAD_DIM), lambda i, h: (h, 0, 0))
    tab = pl.BlockSpec((s, RET_HEAD_DIM), lambda i, h: (0, 0))
    return pl.pallas_call(
        functools.partial(_ret_kernel, ctx=ctx),
        grid=(b, RET_HEADS),
        in_specs=[col(0), col(RET_HEADS), col(2 * RET_HEADS), col(3 * RET_HEADS), tab, tab,
                  pl.BlockSpec((1, 8, RET_HEAD_DIM), lambda i, h: (h, 0, 0)), vec, vec],
        out_specs=pl.BlockSpec((1, s, RET_HEAD_DIM), lambda i, h: (i, 0, h)),
        out_shape=jax.ShapeDtypeStruct((b, s, RET_WIDTH), BF16),
        scratch_shapes=[pltpu.VMEM((s, RET_HEAD_DIM), F32), pltpu.VMEM((s, RET_HEAD_DIM), BF16),
                        pltpu.VMEM((s // CHUNK, 2 * RET_HEAD_DIM, RET_HEAD_DIM), F32),
                        pltpu.VMEM((s // CHUNK, 2 * RET_HEAD_DIM, RET_HEAD_DIM), BF16)],
        compiler_params=_params("parallel", "parallel"),
        name="retention",
    )(p_ret, p_ret, p_ret, p_ret, cos, sin, lg, gw, gb)


def _mla_prep_kernel(m_ref, qn_ref, kn_ref, wq_ref, wk_ref, wvt_ref, cos_ref, sin_ref, q_ref, k_ref, v_ref):
    m = m_ref[0].astype(F32)
    cq = _rms(m[:, :MLA_Q_RANK]) * qn_ref[...]
    ckv = (_rms(m[:, MLA_Q_RANK:MLA_Q_RANK + MLA_KV_RANK]) * kn_ref[...]).astype(BF16)
    kr = m[:, MLA_Q_RANK + MLA_KV_RANK:]
    q = _dot(cq.astype(BF16), wq_ref[...])
    kn = _dot(ckv, wk_ref[...])
    vt = _dot_nt(wvt_ref[...], ckv)
    cos = cos_ref[...]
    sin = sin_ref[...]
    first_half = (_iota(cos.shape, 1) % MLA_ROPE) < (MLA_ROPE // 2)

    def rope(x):
        swapped = jnp.where(first_half, pltpu.roll(x, LANES - MLA_ROPE // 2, 1), pltpu.roll(x, MLA_ROPE // 2, 1))
        return x * cos + swapped * sin

    scale = (MLA_NOPE + MLA_ROPE) ** -0.5
    kr = rope(kr).astype(BF16)
    for h in range(MLA_HEADS):
        c0 = h * MLA_QK_PAD
        q_ref[0, h, :, :MLA_NOPE] = (q[:, c0:c0 + MLA_NOPE] * scale).astype(BF16)
        q_ref[0, h, :, MLA_NOPE:] = (rope(q[:, c0 + MLA_NOPE:c0 + MLA_QK_PAD]) * scale).astype(BF16)
        k_ref[0, h, :, :MLA_NOPE] = kn[:, h * MLA_NOPE:(h + 1) * MLA_NOPE].astype(BF16)
        k_ref[0, h, :, MLA_NOPE:] = kr
        v_ref[0, h] = vt[h * MLA_V:(h + 1) * MLA_V, :].astype(BF16)


def _mla_prep_call(p_mla, qn, kn, wq, wk, wvt, cos, sin):
    b, s, _ = p_mla.shape
    tm = _largest_tile(s, (768, 512, 256))
    head = lambda w: pl.BlockSpec((1, MLA_HEADS, tm, w), lambda i, t: (i, 0, t, 0))
    tab = pl.BlockSpec((tm, LANES), lambda i, t: (t, 0))
    return pl.pallas_call(
        _mla_prep_kernel,
        grid=(b, s // tm),
        in_specs=[pl.BlockSpec((1, tm, SEG_MLA), lambda i, t: (i, t, 0)),
                  _resident(qn.shape), _resident(kn.shape), _resident(wq.shape), _resident(wk.shape),
                  _resident(wvt.shape), tab, tab],
        out_specs=[head(MLA_QK_PAD), head(MLA_QK_PAD),
                   pl.BlockSpec((1, MLA_HEADS, MLA_V, tm), lambda i, t: (i, 0, 0, t))],
        out_shape=[jax.ShapeDtypeStruct((b, MLA_HEADS, s, MLA_QK_PAD), BF16),
                   jax.ShapeDtypeStruct((b, MLA_HEADS, s, MLA_QK_PAD), BF16),
                   jax.ShapeDtypeStruct((b, MLA_HEADS, MLA_V, s), BF16)],
        compiler_params=_params("parallel", "parallel"),
        name="mla_prep",
    )(p_mla, qn, kn, wq, wk, wvt, cos, sin)


def _attn_kernel(q_ref, k_ref, v_ref, o_ref, *, ctx):
    tq = q_ref.shape[2]
    s_len = k_ref.shape[2]
    row0 = pl.program_id(2) * tq

    def attend(start, size, nk):
        rows = slice(start, start + size)
        sc = _dot_nt(k_ref[0, 0, :nk, :], q_ref[0, 0, rows, :])
        p = jnp.exp(sc - jnp.max(sc, axis=0, keepdims=True))
        o_t = _dot(v_ref[0, 0, :, :nk], p.astype(BF16)) * (1.0 / jnp.sum(p, axis=0, keepdims=True))
        o_ref[0, rows, :] = o_t.T.astype(o_ref.dtype)

    def tile(n_ctx):
        if n_ctx:
            attend(0, n_ctx, ctx)
        start = n_ctx
        while start < tq:
            size = min(2 * ATTN_SUB, tq - start)
            attend(start, size, s_len)
            start += size

    pl.when(row0 < ctx)(lambda: tile(ctx))
    pl.when(row0 >= ctx)(lambda: tile(0))


def _attn_call(q, k, v, ctx):
    b, nh, s, _ = q.shape
    tq = _largest_tile(s, (768, 512, 256))
    assert ctx % ATTN_SUB == 0 and ctx <= tq
    return pl.pallas_call(
        functools.partial(_attn_kernel, ctx=ctx),
        grid=(b, nh, s // tq),
        in_specs=[pl.BlockSpec((1, 1, tq, MLA_QK_PAD), lambda i, h, t: (i, h, t, 0)),
                  pl.BlockSpec((1, 1, s, MLA_QK_PAD), lambda i, h, t: (i, h, 0, 0)),
                  pl.BlockSpec((1, 1, MLA_V, s), lambda i, h, t: (i, h, 0, 0))],
        out_specs=pl.BlockSpec((1, tq, MLA_V), lambda i, h, t: (i, t, h)),
        out_shape=jax.ShapeDtypeStruct((b, s, MLA_WIDTH), BF16),
        compiler_params=_params("parallel", "parallel", "arbitrary"),
        name="mla_attention",
    )(q, k, v)


def _padded_row(n, nctx):
    return pl.multiple_of(n * CHUNK + jnp.where(n < nctx, SUBLANES, 2 * SUBLANES), SUBLANES)


def _fill_padded(src_ref, col, pad_scr, *, ctx):
    s_len = src_ref.shape[1]
    width = pad_scr.shape[1]
    nctx = ctx // CHUNK
    zeros = jnp.zeros((SUBLANES, width), F32)
    pad_scr[0:SUBLANES, :] = zeros
    pad_scr[ctx + SUBLANES:ctx + 2 * SUBLANES, :] = zeros
    pad_scr[s_len + 2 * SUBLANES:s_len + 3 * SUBLANES, :] = zeros

    def body(n, carry):
        r = pl.ds(pl.multiple_of(n * CHUNK, CHUNK), CHUNK)
        pad_scr[pl.ds(_padded_row(n, nctx), CHUNK), :] = src_ref[0, r, col].astype(F32)
        return carry

    lax.fori_loop(0, s_len // CHUNK, body, 0)


def _conv4_all(pad_scr, taps_ref, dst_scr, act, *, ctx):
    s_len = dst_scr.shape[0]
    for n in range(s_len // CHUNK):
        base = n * CHUNK + (SUBLANES if n * CHUNK < ctx else 2 * SUBLANES)
        acc = taps_ref[4:5, :] + pad_scr[base:base + CHUNK, :] * taps_ref[1:2, :]
        acc = acc + pad_scr[base - 1:base - 1 + CHUNK, :] * taps_ref[0:1, :]
        acc = acc + pad_scr[base + 1:base + 1 + CHUNK, :] * taps_ref[2:3, :]
        acc = acc + pad_scr[base + 2:base + 2 + CHUNK, :] * taps_ref[3:4, :]
        dst_scr[n * CHUNK:(n + 1) * CHUNK, :] = act(acc)


def _lru_kernel(x_ref, g_ref, w_ref, vec_ref, o_ref, pad_scr, u_scr, af_scr, bf_scr, ab_scr, bb_scr,
                ta_scr, tb_scr, hin_scr, *, ctx):
    s_len = x_ref.shape[1]
    nck = s_len // CHUNK
    ntile, nctile = s_len // SUBLANES, ctx // SUBLANES
    ngrp, ncgrp = ntile // SUBLANES, nctile // SUBLANES
    _fill_padded(x_ref, slice(None), pad_scr, ctx=ctx)
    _conv4_all(pad_scr, vec_ref, u_scr, lambda v: v, ctx=ctx)

    def gates(n, carry):
        r = pl.ds(pl.multiple_of(n * CHUNK, CHUNK), CHUNK)
        u = u_scr[r, :]
        z = _dot(u.astype(BF16), w_ref[0])
        for d, (a_scr, b_scr) in enumerate(((af_scr, bf_scr), (ab_scr, bb_scr))):
            rg = _sigmoid(z[:, (2 * d) * LANES:(2 * d + 1) * LANES] + vec_ref[5 + 2 * d:6 + 2 * d, :])
            ig = _sigmoid(z[:, (2 * d + 1) * LANES:(2 * d + 2) * LANES] + vec_ref[6 + 2 * d:7 + 2 * d, :])
            log_a = -LRU_C * rg * vec_ref[9 + d:10 + d, :]
            a = jnp.exp(log_a)
            a_scr[r, :] = a
            b_scr[r, :] = jnp.sqrt(-jnp.tanh(log_a) * (1.0 + a * a)) * (ig * u)
        return carry

    lax.fori_loop(0, nck, gates, 0, unroll=2)

    part = ntile // 4
    assert part % SUBLANES == 0

    def tile_rows(k, p):
        return pl.ds(p * part * SUBLANES + k, part, stride=SUBLANES)

    for d, (a_scr, b_scr) in enumerate(((af_scr, bf_scr), (ab_scr, bb_scr))):
        ks = range(SUBLANES) if d == 0 else range(SUBLANES - 1, -1, -1)
        for p in range(4):
            pa = qa = None
            for k in ks:
                a_k = a_scr[tile_rows(k, p), :]
                b_k = b_scr[tile_rows(k, p), :]
                if pa is None:
                    pa, qa = a_k, b_k
                else:
                    pa, qa = a_k * pa, a_k * qa + b_k
                    a_scr[tile_rows(k, p), :] = pa
                    b_scr[tile_rows(k, p), :] = qa
            rows = slice(p * part, (p + 1) * part)
            ta_scr[d, rows, :] = pa
            tb_scr[d, rows, :] = qa

    rt = _iota((ntile, LANES), 0) % SUBLANES
    zero = jnp.zeros((SUBLANES, LANES), F32)
    hin_scr[0, 0:SUBLANES, :] = zero
    hin_scr[1, 0:SUBLANES, :] = zero
    for d in range(2):
        pa = ta_scr[d]
        qa = tb_scr[d]
        for sh in (1, 2, 4):
            if d == 0:
                keep = rt >= sh
                a_s, b_s = pltpu.roll(pa, sh, 0), pltpu.roll(qa, sh, 0)
            else:
                keep = rt < SUBLANES - sh
                a_s, b_s = pltpu.roll(pa, ntile - sh, 0), pltpu.roll(qa, ntile - sh, 0)
            qa = qa + pa * jnp.where(keep, b_s, 0.0)
            pa = pa * jnp.where(keep, a_s, 1.0)
        order = range(ngrp) if d == 0 else list(range(ncgrp - 1, -1, -1)) + list(range(ngrp - 1, ncgrp - 1, -1))
        h = zero
        for g in order:
            rows = slice(g * SUBLANES, (g + 1) * SUBLANES)
            t = pa[rows] * h + qa[rows]
            hin_scr[d, SUBLANES + g * SUBLANES:2 * SUBLANES + g * SUBLANES, :] = t
            last = t[SUBLANES - 1:SUBLANES] if d == 0 else t[0:1]
            h = jnp.broadcast_to(last, (SUBLANES, LANES))
    hin_scr[1, SUBLANES + ntile:SUBLANES + ntile + 1, :] = hin_scr[1, SUBLANES:SUBLANES + 1, :]

    tile_id = _iota((part, LANES), 0)
    for d, (a_scr, b_scr) in enumerate(((af_scr, bf_scr), (ab_scr, bb_scr))):
        for p in range(4):
            if d == 0:
                h_in = hin_scr[0, SUBLANES - 1 + p * part:SUBLANES - 1 + (p + 1) * part, :]
            else:
                h_in = hin_scr[1, SUBLANES + 1 + p * part:SUBLANES + 1 + (p + 1) * part, :]
                h_in = jnp.where(tile_id + p * part == nctile - 1, 0.0, h_in)
            for k in range(SUBLANES):
                b_scr[tile_rows(k, p), :] = a_scr[tile_rows(k, p), :] * h_in + b_scr[tile_rows(k, p), :]

    def finish(n, carry):
        r = pl.ds(pl.multiple_of(n * CHUNK, CHUNK), CHUNK)
        g = g_ref[0, r, :].astype(F32)
        o_ref[0, r, :] = ((bf_scr[r, :] + bb_scr[r, :]) * _gelu_tanh(g)).astype(o_ref.dtype)
        return carry

    lax.fori_loop(0, nck, finish, 0)


def _lru_call(p_lru, w_bd, vec, ctx):
    b, s, _ = p_lru.shape
    ncb = LRU_WIDTH // LANES
    return pl.pallas_call(
        functools.partial(_lru_kernel, ctx=ctx),
        grid=(b, ncb),
        in_specs=[pl.BlockSpec((1, s, LANES), lambda i, j: (i, 0, j)),
                  pl.BlockSpec((1, s, LANES), lambda i, j: (i, 0, ncb + j)),
                  pl.BlockSpec((1, LANES, 4 * LANES), lambda i, j: (j, 0, 0)),
                  pl.BlockSpec((16, LANES), lambda i, j: (0, j))],
        out_specs=pl.BlockSpec((1, s, LANES), lambda i, j: (i, 0, j)),
        out_shape=jax.ShapeDtypeStruct((b, s, LRU_WIDTH), BF16),
        scratch_shapes=[pltpu.VMEM((s + 3 * SUBLANES, LANES), F32)] + [pltpu.VMEM((s, LANES), F32)] * 5
        + [pltpu.VMEM((2, s // SUBLANES, LANES), F32)] * 2
        + [pltpu.VMEM((2, s // SUBLANES + 2 * SUBLANES, LANES), F32)],
        compiler_params=_params("parallel", "parallel"),
        name="rg_lru",
    )(p_lru, p_lru, w_bd, vec)


def _expand_heads(m, lane0, rows):
    width = SSD_HG * SSD_HEAD_DIM
    lane = _iota((rows, width), 1)
    out = jnp.broadcast_to(m[:, lane0 + SSD_HG - 1:lane0 + SSD_HG], (rows, width))
    for h in range(SSD_HG - 2, -1, -1):
        out = jnp.where(lane < (h + 1) * SSD_HEAD_DIM, m[:, lane0 + h:lane0 + h + 1], out)
    return out


def _ssd_kernel(x_ref, b_ref, c_ref, z_ref, dt_ref, cvx_ref, cvb_ref, cvc_ref, hp_ref, dsk_ref, o_ref,
                padx, padb, padc, xs_scr, bs_scr, cs_scr, y_scr, mf_scr, mb_scr, ds_scr, cd_scr, st_scr, sf_scr, sb_scr,
                cum_scr, cb_scr, cumt_scr, dtt_scr, *, ctx):
    s_len = x_ref.shape[1]
    nck, nctx = s_len // CHUNK, ctx // CHUNK
    gw = SSD_HG * SSD_HEAD_DIM
    _fill_padded(x_ref, slice(None), padx, ctx=ctx)
    _fill_padded(b_ref, slice(None), padb, ctx=ctx)
    _fill_padded(c_ref, slice(None), padc, ctx=ctx)

    _conv4_all(padx, cvx_ref, xs_scr, _silu, ctx=ctx)
    _conv4_all(padb, cvb_ref, bs_scr, _silu, ctx=ctx)
    _conv4_all(padc, cvc_ref, cs_scr, _silu, ctx=ctx)

    ii = _iota((CHUNK, CHUNK), 0)
    jj = _iota((CHUNK, CHUNK), 1)
    spread = (_iota((LANES, 2 * gw), 0) == _iota((LANES, 2 * gw), 1) // SSD_HEAD_DIM).astype(BF16)

    lane256 = _iota((CHUNK, gw), 1)
    dt_bias = hp_ref[0, 0:1, :]
    a_neg = hp_ref[0, 1:2, :]

    def rows(n):
        return pl.ds(pl.multiple_of(n * CHUNK, CHUNK), CHUNK)

    pair = 2 * CHUNK
    assert nck % 2 == 0
    i2 = _iota((pair, pair), 0)
    j2 = _iota((pair, pair), 1)
    tri2 = ((j2 <= i2) & (j2 // CHUNK == i2 // CHUNK)).astype(BF16)
    first_chunk = _iota((pair, LANES), 0) < CHUNK
    is_fwd_lane = _iota((pair, LANES), 1) < SSD_HG

    def decays(i, carry):
        r2 = pl.ds(pl.multiple_of(i * pair, pair), pair)
        dt = _softplus(dt_ref[0, r2, :] + dt_bias)
        dta = dt * a_neg
        af = _dot2_l(tri2, dta)
        tot = jnp.where(first_chunk, af[CHUNK - 1:CHUNK, :], af[pair - 1:pair, :])
        rv = tot - af + dta
        cum = jnp.where(is_fwd_lane, af, rv)
        cum_scr[r2, :] = cum
        cum_t = cum.T
        dt_t = dt.T
        mult = _dot2_r(jnp.exp(cum), spread)
        mf_scr[r2, :] = mult[:, 0:gw]
        mb_scr[r2, :] = mult[:, gw:2 * gw]
        edge = jnp.where(is_fwd_lane, tot, jnp.where(first_chunk, rv[0:1, :], rv[CHUNK:CHUNK + 1, :]))
        wgt = _dot((jnp.exp(edge - cum) * dt).astype(BF16), spread)
        xg = xs_scr[r2, :]
        xw_f = (xg * wgt[:, 0:gw]).astype(BF16)
        xw_b = (xg * wgt[:, gw:2 * gw]).astype(BF16)
        chunk_decay = jnp.exp(edge)
        for c in range(2):
            n = 2 * i + c
            r = rows(n)
            rc = slice(c * CHUNK, (c + 1) * CHUNK)
            cumt_scr[n] = cum_t[0:SUBLANES, rc]
            dtt_scr[n] = dt_t[0:SUBLANES, rc]
            bsf = bs_scr[r, :]
            cb_scr[r, :] = _dot_nt(cs_scr[r, :].astype(BF16), bsf.astype(BF16))
            bt = bsf.T.astype(BF16)
            ds_scr[n, :, 0:gw] = _dot(bt, xw_f[rc])
            ds_scr[n, :, gw:2 * gw] = _dot(bt, xw_b[rc])
            cd_scr[n] = jnp.broadcast_to(chunk_decay[c * CHUNK:c * CHUNK + 1, :], (SUBLANES, LANES))
        return carry

    lax.fori_loop(0, nck // 2, decays, 0)

    def intra(n, carry):
        r = rows(n)
        cum = cum_scr[r, :]
        cum_t = cumt_scr[n]
        dt_t = dtt_scr[n]
        cb = cb_scr[r, :]
        xg = xs_scr[r, :]
        y = jnp.zeros((CHUNK, gw), F32)
        for h in range(SSD_HG):
            hb = SSD_HG + h
            arg = jnp.where(ii >= jj, cum[:, h:h + 1] - cum_t[h:h + 1, :], cum[:, hb:hb + 1] - cum_t[hb:hb + 1, :])
            wdt = jnp.where(ii > jj, dt_t[h:h + 1, :],
                            jnp.where(ii < jj, dt_t[hb:hb + 1, :], dt_t[h:h + 1, :] + dt_t[hb:hb + 1, :]))
            w = (cb * jnp.exp(arg) * wdt).astype(BF16)
            in_head = (lane256 >= h * SSD_HEAD_DIM) & (lane256 < (h + 1) * SSD_HEAD_DIM)
            y = y + _dot(w, jnp.where(in_head, xg, 0.0).astype(BF16))
        y_scr[r, :] = y
        return carry

    lax.fori_loop(0, nck, intra, 0, unroll=2)

    sf_scr[...] = jnp.zeros(sf_scr.shape, F32)
    sb_scr[...] = jnp.zeros(sb_scr.shape, F32)

    def states(i, carry):
        sf = sf_scr[...]
        st_scr[i, :, 0:gw] = sf.astype(BF16)
        sf_scr[...] = sf * _expand_heads(cd_scr[i, 0:1, :], 0, 1) + ds_scr[i, :, 0:gw]
        j = jnp.where(i < nctx, nctx - 1 - i, nck - 1 - (i - nctx))
        sb = sb_scr[...]
        st_scr[j, :, gw:2 * gw] = sb.astype(BF16)
        sb_scr[...] = sb * _expand_heads(cd_scr[j, 0:1, :], SSD_HG, 1) + ds_scr[j, :, gw:2 * gw]
        return carry

    lax.fori_loop(0, nck, states, 0)

    def inter(n, carry):
        r = rows(n)
        ys = _dot(cs_scr[r, :].astype(BF16), st_scr[n])
        y = y_scr[r, :] + ys[:, 0:gw] * mf_scr[r, :] + ys[:, gw:2 * gw] * mb_scr[r, :] + dsk_ref[...] * xs_scr[r, :]
        o_ref[0, r, :] = (y * _silu(z_ref[0, r, :].astype(F32))).astype(o_ref.dtype)
        return carry

    lax.fori_loop(0, nck, inter, 0, unroll=2)


def _ssd_call(p_ssd, dt, cv, hp, dsk, ctx):
    b, s, _ = p_ssd.shape
    gw = SSD_HG * SSD_HEAD_DIM
    nx = SSD_WIDTH // LANES
    nbc = SSD_GROUPS * SSD_STATE // LANES
    pad = lambda w: pltpu.VMEM((s + 3 * SUBLANES, w), F32)
    return pl.pallas_call(
        functools.partial(_ssd_kernel, ctx=ctx),
        grid=(b, SSD_GROUPS),
        in_specs=[pl.BlockSpec((1, s, gw), lambda i, g: (i, 0, g)),
                  pl.BlockSpec((1, s, LANES), lambda i, g: (i, 0, nx + g)),
                  pl.BlockSpec((1, s, LANES), lambda i, g: (i, 0, nx + nbc + g)),
                  pl.BlockSpec((1, s, gw), lambda i, g: (i, 0, SSD_XBC // gw + g)),
                  pl.BlockSpec((1, s, LANES), lambda i, g: (i, 0, g)),
                  pl.BlockSpec((8, gw), lambda i, g: (0, g)),
                  pl.BlockSpec((8, LANES), lambda i, g: (0, nx + g)),
                  pl.BlockSpec((8, LANES), lambda i, g: (0, nx + nbc + g)),
                  pl.BlockSpec((1, 8, LANES), lambda i, g: (g, 0, 0)),
                  pl.BlockSpec((1, gw), lambda i, g: (0, g))],
        out_specs=pl.BlockSpec((1, s, gw), lambda i, g: (i, 0, g)),
        out_shape=jax.ShapeDtypeStruct((b, s, SSD_WIDTH), BF16),
        scratch_shapes=[pad(gw), pad(LANES), pad(LANES),
                        pltpu.VMEM((s, gw), F32), pltpu.VMEM((s, LANES), F32), pltpu.VMEM((s, LANES), F32),
                        pltpu.VMEM((s, gw), F32), pltpu.VMEM((s, gw), F32), pltpu.VMEM((s, gw), F32),
                        pltpu.VMEM((s // CHUNK, SSD_STATE, 2 * gw), F32), pltpu.VMEM((s // CHUNK, SUBLANES, LANES), F32),
                        pltpu.VMEM((s // CHUNK, SSD_STATE, 2 * gw), BF16),
                        pltpu.VMEM((SSD_STATE, gw), F32), pltpu.VMEM((SSD_STATE, gw), F32),
                        pltpu.VMEM((s, LANES), F32), pltpu.VMEM((s, LANES), F32),
                        pltpu.VMEM((s // CHUNK, SUBLANES, LANES), F32), pltpu.VMEM((s // CHUNK, SUBLANES, LANES), F32)],
        compiler_params=_params("parallel", "parallel"),
        name="ssd",
    )(p_ssd, p_ssd, p_ssd, p_ssd, dt, cv, cv, cv, hp, dsk)


def _merge_kernel(g_ref, ret_ref, mla_ref, lru_ref, ssd_ref, h_ref, mx_ref, mc_ref, wb_ref, wo_ref, vec_ref, nw_ref,
                  o_ref, *, ctx, first_tile, alpha):
    tm = h_ref.shape[1]
    is_ctx = (pl.program_id(1) + first_tile) * tm + _iota((tm, 1), 0) < ctx
    ssd = (_rms(ssd_ref[0].astype(F32)) * nw_ref[...]).astype(BF16)
    acc = None
    for i, br in enumerate((ret_ref[0], mla_ref[0], lru_ref[0], ssd)):
        gate = _sigmoid(g_ref[0, :, i * D_MODEL:(i + 1) * D_MODEL].astype(F32))
        term = gate * _dot(br, wb_ref[i])
        acc = term if acc is None else acc + term
    o = _dot(acc.astype(BF16), wo_ref[...])
    y = _ln(alpha * h_ref[0] + _pick_mod(is_ctx, mx_ref, mc_ref, 2) * o)
    o_ref[0] = y * vec_ref[0:1, :] + vec_ref[1:2, :]


def _merge_call(p_gates, y_ret, y_mla, y_lru, y_ssd, h, mods, wb, wo, vec, nw, ctx, with_ctx, alpha):
    b, s, _ = h.shape
    tm = _largest_tile(s, (768, 512, 256)) if with_ctx else ROW_TILE
    first = 0 if with_ctx else ctx // tm
    nb = mods.shape[0] - 1
    tok = lambda w: pl.BlockSpec((1, tm, w), lambda i, t: (i, t + first, 0))
    return pl.pallas_call(
        functools.partial(_merge_kernel, ctx=ctx, first_tile=first, alpha=alpha),
        grid=(b, s // tm - first),
        in_specs=[tok(SEG_GATES), tok(RET_WIDTH), tok(MLA_WIDTH), tok(LRU_WIDTH), tok(SSD_WIDTH), tok(D_MODEL),
                  pl.BlockSpec((1, 8, D_MODEL), lambda i, t: (i, 0, 0)),
                  pl.BlockSpec((1, 8, D_MODEL), lambda i, t: (nb, 0, 0)),
                  _resident(wb.shape), _resident(wo.shape), _resident(vec.shape), _resident(nw.shape)],
        out_specs=pl.BlockSpec((1, tm, D_MODEL), lambda i, t: (i, t, 0)),
        out_shape=jax.ShapeDtypeStruct((b, s - first * tm, D_MODEL), F32),
        compiler_params=_params("parallel", "parallel"),
        name="merge_postnorm",
    )(p_gates, y_ret, y_mla, y_lru, y_ssd, h, mods, mods, wb, wo, vec, nw)


def _ffn_kernel(h_ref, hp_ref, hn_ref, mx_ref, mc_ref, wu_ref, wd_ref, cv_ref, vec_ref, o_ref, *, ctx, s_len, alpha):
    tm = h_ref.shape[1]
    t = pl.program_id(1)
    row0 = t * tm
    h = h_ref[0]
    in_ctx = row0 + _iota((tm, 1), 0) < ctx

    def pick(row, is_ctx):
        if ctx == 0:
            return mx_ref[0, row:row + 1, :]
        return jnp.where(is_ctx, mc_ref[0, row:row + 1, :], mx_ref[0, row:row + 1, :])

    def mod(x, is_ctx):
        return _ln(x) * (1.0 + pick(4, is_ctx)) + pick(3, is_ctx)

    prev_ok = jnp.logical_and(row0 != 0, row0 != ctx)
    next_ok = jnp.logical_and(row0 + tm != ctx, row0 + tm != s_len)
    xm = jnp.concatenate([jnp.where(prev_ok, mod(hp_ref[0], row0 - 1 < ctx), 0.0), mod(h, in_ctx),
                          jnp.where(next_ok, mod(hn_ref[0], row0 + tm < ctx), 0.0)], axis=0).astype(BF16)
    rows = tm + 2 * SUBLANES
    lo, hi = SUBLANES, SUBLANES + tm
    seam = ctx % tm
    seam_tile = ctx // tm
    seam_row = _iota((2 * SUBLANES, 1), 0)

    def conv(u, c0):
        w_prev, w_mid, w_next = (cv_ref[k:k + 1, c0:c0 + FF_CHUNK] for k in range(3))
        out = cv_ref[3:4, c0:c0 + FF_CHUNK] + u[lo:hi] * w_mid
        out = out + pltpu.roll(u, 1, 0)[lo:hi] * w_prev
        out = out + pltpu.roll(u, rows - 1, 0)[lo:hi] * w_next
        if seam:
            slab = u[seam:seam + 2 * SUBLANES]
            fix = (jnp.where(seam_row == SUBLANES - 1, pltpu.roll(slab, 2 * SUBLANES - 1, 0) * w_next, 0.0)
                   + jnp.where(seam_row == SUBLANES, pltpu.roll(slab, 1, 0) * w_prev, 0.0))
            fix = jnp.where(t == seam_tile, fix, 0.0)
            out = jnp.concatenate([out[:seam - SUBLANES], out[seam - SUBLANES:seam + SUBLANES] - fix,
                                   out[seam + SUBLANES:]], axis=0)
        return out

    acc = jnp.zeros((tm, D_MODEL), F32)
    for c0 in range(0, D_FF, FF_CHUNK):
        g = conv(_dot(xm, wu_ref[:, c0:c0 + FF_CHUNK]), c0)
        v = conv(_dot(xm, wu_ref[:, D_FF + c0:D_FF + c0 + FF_CHUNK]), D_FF + c0)
        acc = acc + _dot((_silu(g) * v).astype(BF16), wd_ref[c0:c0 + FF_CHUNK, :])
    y = _ln(alpha * h + pick(5, in_ctx) * acc)
    o_ref[0] = y * vec_ref[0:1, :] + vec_ref[1:2, :]


def _ffn_call(h, mods, wu, wd, cv, vec, ctx, alpha):
    b, s, _ = h.shape
    tm = _largest_tile(s, (768, 512, 256))
    assert ctx % SUBLANES == 0 and (ctx % tm == 0 or SUBLANES <= ctx % tm <= tm - SUBLANES)
    nb = mods.shape[0] - 1
    per = tm // SUBLANES
    hi_blk = s // SUBLANES - 1
    return pl.pallas_call(
        functools.partial(_ffn_kernel, ctx=ctx, s_len=s, alpha=alpha),
        grid=(b, s // tm),
        in_specs=[pl.BlockSpec((1, tm, D_MODEL), lambda i, t: (i, t, 0)),
                  pl.BlockSpec((1, SUBLANES, D_MODEL), lambda i, t: (i, jnp.maximum(t * per - 1, 0), 0)),
                  pl.BlockSpec((1, SUBLANES, D_MODEL), lambda i, t: (i, jnp.minimum((t + 1) * per, hi_blk), 0)),
                  pl.BlockSpec((1, 8, D_MODEL), lambda i, t: (i, 0, 0)),
                  pl.BlockSpec((1, 8, D_MODEL), lambda i, t: (nb, 0, 0)),
                  _resident(wu.shape), _resident(wd.shape), _resident(cv.shape), _resident(vec.shape)],
        out_specs=pl.BlockSpec((1, tm, D_MODEL), lambda i, t: (i, t, 0)),
        out_shape=jax.ShapeDtypeStruct((b, s, D_MODEL), F32),
        compiler_params=_params("parallel", "parallel"),
        name="conv_ffn_postnorm",
    )(h, h, h, mods, mods, wu, wd, cv, vec)


def _pad_rows(a, rows):
    return jnp.pad(a, ((0, rows - a.shape[0]),) + ((0, 0),) * (a.ndim - 1))


def _rope_tables(t_len, ctx):
    rows = t_len // GRID_W
    r, col = jnp.meshgrid(jnp.arange(rows, dtype=F32), jnp.arange(GRID_W, dtype=F32), indexing='ij')

    def tables(dim):
        quarter = dim // 4
        inv = ROPE_BASE ** (-jnp.arange(quarter, dtype=F32) / quarter)
        ang = jnp.concatenate([r.reshape(-1, 1) * inv, col.reshape(-1, 1) * inv], axis=-1)
        cos, sin = jnp.cos(ang), jnp.sin(ang)
        return jnp.concatenate([cos, cos], -1), jnp.concatenate([-sin, sin], -1)

    def with_ctx(cos, sin, width):
        cos = jnp.pad(cos, ((0, 0), (0, width - cos.shape[1])), constant_values=1.0)
        sin = jnp.pad(sin, ((0, 0), (0, width - sin.shape[1])))
        return (jnp.concatenate([jnp.ones((ctx, width), F32), cos], 0),
                jnp.concatenate([jnp.zeros((ctx, width), F32), sin], 0))

    return with_ctx(*tables(RET_HEAD_DIM), RET_HEAD_DIM), with_ctx(*tables(MLA_ROPE), LANES)


def _layer_params(l, p):
    w_in = p['w_in'][l]
    offs = np.cumsum((0, SEG_GATES, RET_WIDTH, RET_WIDTH, RET_WIDTH, RET_WIDTH, MLA_Q_RANK, MLA_KV_RANK, MLA_ROPE,
                      LRU_WIDTH, LRU_WIDTH, SSD_WIDTH, SSD_XBC, 2 * SSD_HEADS))
    piece = lambda i: w_in[:, offs[i]:offs[i + 1]]
    zeros = lambda n: jnp.zeros((D_MODEL, n), w_in.dtype)
    w_cat = jnp.concatenate([piece(0), piece(1), piece(2), piece(3), piece(4), piece(11), piece(10), piece(8), piece(9),
                             piece(5), piece(6), piece(7), zeros(SEG_MLA - MLA_Q_RANK - MLA_KV_RANK - MLA_ROPE)], axis=1)
    dt_w = piece(12).reshape(D_MODEL, 2, SSD_GROUPS, SSD_HG).transpose(0, 2, 1, 3).reshape(D_MODEL, SSD_GROUPS, 2 * SSD_HG)
    w_dt = jnp.pad(dt_w, ((0, 0), (0, 0), (0, LANES - 2 * SSD_HG))).reshape(D_MODEL, DT_COLS)

    def per_group(v):
        v = v.reshape(2, SSD_GROUPS, SSD_HG).transpose(1, 0, 2).reshape(SSD_GROUPS, 2 * SSD_HG)
        return jnp.pad(v, ((0, 0), (0, LANES - 2 * SSD_HG)))

    ssd_hp = jnp.stack([per_group(p['ssd_dt_bias'][l]), per_group(-jnp.exp(p['ssd_a_log'][l].astype(F32)))], axis=1)
    ssd_hp = jnp.pad(ssd_hp, ((0, 0), (0, 6), (0, 0)))

    wq = p['mla_w_uq'][l].reshape(MLA_Q_RANK, MLA_HEADS, MLA_NOPE + MLA_ROPE)
    wq = jnp.pad(wq, ((0, 0), (0, 0), (0, MLA_QK_PAD - MLA_NOPE - MLA_ROPE))).reshape(MLA_Q_RANK, MLA_HEADS * MLA_QK_PAD)

    wkv = p['mla_w_ukv'][l].reshape(MLA_KV_RANK, MLA_HEADS, MLA_NOPE + MLA_V)

    gw = p['lru_gate_w'][l].reshape(4, LRU_WIDTH // LANES, 2, LRU_BLOCK, LRU_BLOCK)
    w_bd = jnp.einsum('ajpcd,pq->japcqd', gw, jnp.eye(2, dtype=gw.dtype)).reshape(LRU_WIDTH // LANES, 4, LANES, LANES)
    w_bd = w_bd.transpose(0, 2, 1, 3).reshape(LRU_WIDTH // LANES, LANES, 4 * LANES)
    lru_vec = jnp.concatenate([p['lru_conv_w'][l], p['lru_conv_b'][l][None], p['lru_gate_b'][l].reshape(4, LRU_WIDTH),
                               jax.nn.softplus(-p['lru_lambda'][l].astype(F32))], axis=0)

    log_g = jax.nn.log_sigmoid(p['ret_decay'][l].astype(F32))
    ret_lg = jnp.broadcast_to(_pad_rows(log_g, 8).T[:, :, None], (RET_HEADS, 8, RET_HEAD_DIM))

    return dict(
        ada_w=p['ada_w'][l].astype(BF16), ada_b=p['ada_b'][l][None],
        w_cat=w_cat.astype(BF16), w_dt=w_dt.astype(BF16),
        ret_lg=ret_lg, ret_gw=p['ret_gn_w'][l].reshape(RET_HEADS, 1, RET_HEAD_DIM),
        ret_gb=p['ret_gn_b'][l].reshape(RET_HEADS, 1, RET_HEAD_DIM),
        mla_qn=p['mla_q_norm'][l][None], mla_kn=p['mla_kv_norm'][l][None],
        mla_wq=wq.astype(BF16), mla_wk=wkv[:, :, :MLA_NOPE].reshape(MLA_KV_RANK, MLA_HEADS * MLA_NOPE).astype(BF16),
        mla_wvt=wkv[:, :, MLA_NOPE:].reshape(MLA_KV_RANK, MLA_HEADS * MLA_V).T.astype(BF16),
        lru_w=w_bd.astype(BF16), lru_vec=_pad_rows(lru_vec, 16),
        ssd_cv=_pad_rows(jnp.concatenate([p['ssd_conv_w'][l], p['ssd_conv_b'][l][None]], 0), 8),
        ssd_hp=ssd_hp, ssd_dsk=jnp.repeat(p['ssd_d'][l].astype(F32), SSD_HEAD_DIM)[None],
        ssd_nw=p['ssd_norm_w'][l][None],
        w_branch=p['w_branch'][l].astype(BF16), w_out=p['w_out'][l].astype(BF16),
        ln1=_pad_rows(jnp.stack([p['ln1_w'][l], p['ln1_b'][l]]), 8),
        ffn_wu=p['ffn_w_up'][l].astype(BF16), ffn_wd=p['ffn_w_down'][l].astype(BF16),
        ffn_cv=_pad_rows(jnp.concatenate([p['ffn_conv_w'][l], p['ffn_conv_b'][l][None]], 0), 8),
        ln2=_pad_rows(jnp.stack([p['ln2_w'][l], p['ln2_b'][l]]), 8),
    )


def kernel(x, c, ctx, c_ctx, ada_w, ada_b, w_in, ret_decay, ret_gn_w, ret_gn_b, mla_q_norm, mla_w_uq, mla_kv_norm, mla_w_ukv, lru_conv_w, lru_conv_b, lru_gate_w, lru_gate_b, lru_lambda, ssd_conv_w, ssd_conv_b, ssd_dt_bias, ssd_a_log, ssd_d, ssd_norm_w, w_branch, w_out, ln1_w, ln1_b, ffn_w_up, ffn_conv_w, ffn_conv_b, ffn_w_down, ln2_w, ln2_b):
    p = dict(ada_w=ada_w, ada_b=ada_b, w_in=w_in, ret_decay=ret_decay, ret_gn_w=ret_gn_w, ret_gn_b=ret_gn_b,
             mla_q_norm=mla_q_norm, mla_w_uq=mla_w_uq, mla_kv_norm=mla_kv_norm, mla_w_ukv=mla_w_ukv,
             lru_conv_w=lru_conv_w, lru_conv_b=lru_conv_b, lru_gate_w=lru_gate_w, lru_gate_b=lru_gate_b,
             lru_lambda=lru_lambda, ssd_conv_w=ssd_conv_w, ssd_conv_b=ssd_conv_b, ssd_dt_bias=ssd_dt_bias,
             ssd_a_log=ssd_a_log, ssd_d=ssd_d, ssd_norm_w=ssd_norm_w, w_branch=w_branch, w_out=w_out,
             ln1_w=ln1_w, ln1_b=ln1_b, ffn_w_up=ffn_w_up, ffn_conv_w=ffn_conv_w, ffn_conv_b=ffn_conv_b,
             ffn_w_down=ffn_w_down, ln2_w=ln2_w, ln2_b=ln2_b)
    batch, t_len, _ = x.shape
    n_ctx = ctx.shape[1]
    depth = ada_w.shape[0]
    assert n_ctx % ROW_TILE == 0 and t_len % ROW_TILE == 0 and t_len % GRID_W == 0
    alpha = (2 * depth) ** 0.25
    (ret_cos, ret_sin), (mla_cos, mla_sin) = _rope_tables(t_len, n_ctx)
    cond = _pad_rows(jnp.concatenate([c, c_ctx[None]], axis=0), -(-(batch + 1) // 8) * 8)
    h = jnp.concatenate([ctx, x], axis=1)
    for l in range(depth):
        with_ctx = l < depth - 1
        w = _layer_params(l, p)
        mods = _mod_call(cond, w['ada_w'], w['ada_b'])[:batch + 1].reshape(batch + 1, 6, D_MODEL)
        mods = jnp.pad(mods, ((0, 0), (0, 2), (0, 0)))
        p_gates, p_ret, p_ssd, p_lru, p_mla, p_dt = _inproj_call(h, mods, w['w_cat'], w['w_dt'], n_ctx)
        y_ret = _ret_call(p_ret, ret_cos, ret_sin, w['ret_lg'], w['ret_gw'], w['ret_gb'], n_ctx)
        q, k, v = _mla_prep_call(p_mla, w['mla_qn'], w['mla_kn'], w['mla_wq'], w['mla_wk'], w['mla_wvt'], mla_cos, mla_sin)
        y_mla = _attn_call(q, k, v, n_ctx)
        y_lru = _lru_call(p_lru, w['lru_w'], w['lru_vec'], n_ctx)
        y_ssd = _ssd_call(p_ssd, p_dt, w['ssd_cv'], w['ssd_hp'], w['ssd_dsk'], n_ctx)
        h1 = _merge_call(p_gates, y_ret, y_mla, y_lru, y_ssd, h, mods, w['w_branch'], w['w_out'], w['ln1'], w['ssd_nw'],
                         n_ctx, with_ctx, alpha)
        h = _ffn_call(h1, mods, w['ffn_wu'], w['ffn_wd'], w['ffn_cv'], w['ln2'], n_ctx if with_ctx else 0, alpha)
    return h
```
